```python
import math
import jax
import jax.numpy as jnp
from jax import lax
import numpy as np

D_MODEL = 1024
BATCH = 1
SEQ = 16384
DEPTH = 4
DEC_BATCH = 16
DEC_SEQ = 2048
PAST_LEN = 128

N_MIXERS = 2
HEAD_DIM = 64
ROPE_THETA = 10000.0
NORM_EPS = 1e-6
NEG_INF = -1e30
DILATED_GROUPS = ((128, 1), (512, 4), (2048, 16))
N_GROUPS_A = 3
HEADS_PER_GROUP_A = 8
QKV_A = 3 * N_GROUPS_A * HEADS_PER_GROUP_A * HEAD_DIM
OUT_A = HEADS_PER_GROUP_A * HEAD_DIM
N_HEADS_B = D_MODEL // HEAD_DIM
QKV_B = 3 * N_HEADS_B * HEAD_DIM
OUT_B = N_HEADS_B * HEAD_DIM
GRID_W = 64
NA_ROWS = 8
NA_COLS = 16
NA_QCOLS = 16
NA_KCOLS = 32
N_EXPERTS = 32
TOP_K = 4
D_FF = D_MODEL
SWIGLU_LIMIT = 7.0
SWIGLU_ALPHA = 1.702
MOE_BLOCK = 128

kernel_name = 'hybrid_dilated_neighbourhood_moe_encoder'


def rms_norm(x, g):
    xf = x.astype(jnp.float32)
    y = xf * lax.rsqrt(jnp.mean(xf * xf, axis=-1, keepdims=True) + NORM_EPS)
    return (y * g.astype(jnp.float32)).astype(x.dtype)


def rope(x):
    s = x.shape[1]
    half = HEAD_DIM // 2
    inv = ROPE_THETA ** (-jnp.arange(half, dtype=jnp.float32) / half)
    ang = jnp.arange(s, dtype=jnp.float32)[:, None] * inv[None, :]
    ang = ang.reshape((1, s) + (1,) * (x.ndim - 3) + (half,))
    cos, sin = jnp.cos(ang), jnp.sin(ang)
    xf = x.astype(jnp.float32)
    x1, x2 = xf[..., :half], xf[..., half:]
    out = jnp.concatenate([x1 * cos - x2 * sin, x2 * cos + x1 * sin], axis=-1)
    return out.astype(x.dtype)


def modulate(x, shift, scale):
    return x * (1 + scale[:, None, :]) + shift[:, None, :]


def banded_attention(q, k, v, radius):
    n, L, h, dh = q.shape
    nb = -(-L // radius)
    lp = nb * radius
    pad = lp - L
    qb = jnp.pad(q, ((0, 0), (0, pad), (0, 0), (0, 0))).reshape(n, nb, radius, h, dh)

    def key_blocks(t):
        tp = jnp.pad(t, ((0, 0), (radius, pad + radius), (0, 0), (0, 0)))
        tp = tp.reshape(n, nb + 2, radius, h, dh)
        return jnp.concatenate([tp[:, :nb], tp[:, 1:nb + 1], tp[:, 2:]], axis=2)

    kb, vb = key_blocks(k), key_blocks(v)
    qpos = np.arange(lp).reshape(nb, radius)
    kpos = qpos[:, :1] - radius + np.arange(3 * radius)[None, :]
    valid = ((np.abs(qpos[:, :, None] - kpos[:, None, :]) <= radius)
             & (kpos[:, None, :] >= 0) & (kpos[:, None, :] < L))
    s = jnp.einsum('nbqhd,nbkhd->nhbqk', qb, kb).astype(jnp.float32) * (dh ** -0.5)
    s = jnp.where(valid[None, None], s, NEG_INF)
    m = jnp.max(s, axis=-1, keepdims=True)
    p = jnp.exp(s - m)
    den = jnp.sum(p, axis=-1, keepdims=True)
    o = jnp.einsum('nhbqk,nbkhd->nbqhd', (p / den).astype(v.dtype), vb)
    lse = (m + jnp.log(den))[..., 0]
    o = o.reshape(n, lp, h, dh)[:, :L]
    lse = jnp.transpose(lse, (0, 2, 3, 1)).reshape(n, lp, h)[:, :L]
    return o, lse


def dilated_mixer(h, w_qkv, q_gain, k_gain, w_o):
    b, s, _ = h.shape
    qkv = (h @ w_qkv).reshape(b, s, 3, N_GROUPS_A, HEADS_PER_GROUP_A, HEAD_DIM)
    q = rope(rms_norm(qkv[:, :, 0], q_gain))
    k = rope(rms_norm(qkv[:, :, 1], k_gain))
    v = qkv[:, :, 2]
    outs, lses = [], []
    for g, (window, dil) in enumerate(DILATED_GROUPS):
        radius = window // (2 * dil)
        sub = s // dil

        def by_phase(t):
            t = t[:, :, g].reshape(b, sub, dil, HEADS_PER_GROUP_A, HEAD_DIM)
            return jnp.swapaxes(t, 1, 2).reshape(b * dil, sub, HEADS_PER_GROUP_A, HEAD_DIM)

        o, lse = banded_attention(by_phase(q), by_phase(k), by_phase(v), radius)
        o = jnp.swapaxes(o.reshape(b, dil, sub, HEADS_PER_GROUP_A, HEAD_DIM), 1, 2)
        lse = jnp.swapaxes(lse.reshape(b, dil, sub, HEADS_PER_GROUP_A), 1, 2)
        outs.append(o.reshape(b, s, HEADS_PER_GROUP_A, HEAD_DIM))
        lses.append(lse.reshape(b, s, HEADS_PER_GROUP_A))
    wts = jax.nn.softmax(jnp.stack(lses, axis=0), axis=0)
    o = jnp.sum(wts[..., None].astype(v.dtype) * jnp.stack(outs, axis=0), axis=0)
    return o.reshape(b, s, OUT_A) @ w_o


def neighbourhood_mixer(h, w_qkv, q_gain, k_gain, rpb, w_o):
    b, s, _ = h.shape
    rows = s // GRID_W
    wr = min(NA_ROWS, rows)
    qkv = (h @ w_qkv).reshape(b, rows, GRID_W, 3, N_HEADS_B, HEAD_DIM)
    q = rms_norm(qkv[:, :, :, 0], q_gain)
    k = rms_norm(qkv[:, :, :, 1], k_gain)
    v = qkv[:, :, :, 2]
    n_cb = GRID_W // NA_QCOLS
    qcol = np.arange(GRID_W).reshape(n_cb, NA_QCOLS)
    kstart = np.clip(qcol[:, 0] - NA_COLS // 2, 0, GRID_W - NA_KCOLS)
    kcol = kstart[:, None] + np.arange(NA_KCOLS)[None, :]
    cstart = np.clip(qcol - NA_COLS // 2, 0, GRID_W - NA_COLS)
    col_valid = ((kcol[:, None, :] >= cstart[..., None])
                 & (kcol[:, None, :] < cstart[..., None] + NA_COLS))
    col_off = np.clip(kcol[:, None, :] - qcol[..., None] + NA_COLS - 1, 0, 2 * NA_COLS - 2)
    col_bias = rpb[:, :, col_off]
    col_mask = jnp.where(col_valid, 0.0, NEG_INF).astype(jnp.float32)
    scale = HEAD_DIM ** -0.5

    def one_row(r):
        rs = jnp.clip(r - wr // 2, 0, rows - wr)
        kr = lax.dynamic_slice_in_dim(k, rs, wr, axis=1)[:, :, kcol]
        vr = lax.dynamic_slice_in_dim(v, rs, wr, axis=1)[:, :, kcol]
        qr = lax.dynamic_index_in_dim(q, r, axis=1, keepdims=False)
        qr = qr.reshape(b, n_cb, NA_QCOLS, N_HEADS_B, HEAD_DIM)
        sc = jnp.einsum('bcqhd,brckhd->bhcqrk', qr, kr).astype(jnp.float32) * scale
        row_off = rs + jnp.arange(wr) - r + NA_ROWS - 1
        bias = jnp.transpose(jnp.take(col_bias, row_off, axis=1), (0, 2, 3, 1, 4))
        sc = sc + bias.astype(jnp.float32) + col_mask[:, :, None, :]
        sc = sc.reshape(b, N_HEADS_B, n_cb, NA_QCOLS, wr * NA_KCOLS)
        p = jax.nn.softmax(sc, axis=-1).reshape(b, N_HEADS_B, n_cb, NA_QCOLS, wr, NA_KCOLS)
        o = jnp.einsum('bhcqrk,brckhd->bcqhd', p.astype(v.dtype), vr)
        return o.reshape(b, GRID_W, OUT_B)

    out = lax.map(one_row, jnp.arange(rows))
    out = jnp.moveaxis(out, 0, 1).reshape(b, s, OUT_B)
    return out @ w_o


def clamped_swiglu(hgu):
    gate = jnp.minimum(hgu[..., :D_FF], SWIGLU_LIMIT)
    up = jnp.clip(hgu[..., D_FF:], -SWIGLU_LIMIT, SWIGLU_LIMIT)
    return (up + 1) * gate * jax.nn.sigmoid(SWIGLU_ALPHA * gate)


def moe(h, w_router, b_router, w_gu, b_gu, w_dn, b_dn):
    b, s, d = h.shape
    x = h.reshape(-1, d)
    n = x.shape[0]
    logits = (x @ w_router + b_router).astype(jnp.float32)
    top_val, top_idx = lax.top_k(logits, TOP_K)
    gates = jax.nn.softmax(top_val, axis=-1)
    e_flat = top_idx.reshape(-1)
    tok_flat = jnp.repeat(jnp.arange(n, dtype=jnp.int32), TOP_K)
    g_flat = gates.reshape(-1)
    order = jnp.argsort(e_flat)
    se = e_flat[order]
    counts = jnp.bincount(e_flat, length=N_EXPERTS)
    starts = jnp.cumsum(counts) - counts
    padded = (counts + MOE_BLOCK - 1) // MOE_BLOCK * MOE_BLOCK
    pad_ends = jnp.cumsum(padded)
    pad_starts = pad_ends - padded
    dest = pad_starts[se] + (jnp.arange(n * TOP_K) - starts[se])
    n_blocks = -(-(n * TOP_K) // MOE_BLOCK) + N_EXPERTS
    cap = n_blocks * MOE_BLOCK
    slot_tok = jnp.full((cap,), n, jnp.int32).at[dest].set(tok_flat[order])
    slot_gate = jnp.zeros((cap,), jnp.float32).at[dest].set(g_flat[order])
    blk_expert = jnp.minimum(
        jnp.searchsorted(pad_ends, jnp.arange(n_blocks) * MOE_BLOCK, side='right'), N_EXPERTS - 1)
    x_pad = jnp.concatenate([x, jnp.zeros((1, d), x.dtype)], axis=0)
    xs = x_pad[slot_tok].reshape(n_blocks, MOE_BLOCK, d)

    def expert_block(args):
        xb, e = args
        return clamped_swiglu(xb @ w_gu[e] + b_gu[e]) @ w_dn[e] + b_dn[e]

    yb = lax.map(expert_block, (xs, blk_expert)).reshape(cap, d)
    y = jnp.zeros((n + 1, d), x.dtype).at[slot_tok].add(yb * slot_gate[:, None].astype(yb.dtype))
    return y[:n].reshape(b, s, d)


def trunk(x, c, norm_g, w_mod, b_mod, a_w_qkv, a_q_gain, a_k_gain, a_w_o,
          b_w_qkv, b_q_gain, b_k_gain, b_rpb, b_w_o,
          w_router, b_router, w_gu, b_gu, w_dn, b_dn):
    c_act = jax.nn.silu(c)
    for layer in range(DEPTH):
        mod = c_act @ w_mod[layer] + b_mod[layer]
        sh1, sc1, g1, sh2, sc2, g2 = jnp.split(mod, 6, axis=-1)
        hmix = modulate(rms_norm(x, norm_g[layer, 0]), sh1, sc1)
        j = layer // N_MIXERS
        if layer % N_MIXERS == 0:
            mix = dilated_mixer(hmix, a_w_qkv[j], a_q_gain[j], a_k_gain[j], a_w_o[j])
        else:
            mix = neighbourhood_mixer(hmix, b_w_qkv[j], b_q_gain[j], b_k_gain[j], b_rpb[j], b_w_o[j])
        x = x + g1[:, None, :] * mix
        hff = modulate(rms_norm(x, norm_g[layer, 1]), sh2, sc2)
        x = x + g2[:, None, :] * moe(hff, w_router[layer], b_router[layer], w_gu[layer],
                                     b_gu[layer], w_dn[layer], b_dn[layer])
    return x


def setup_inputs(seed: int = 0) -> dict:
    key = jax.random.key(seed)
    ks = jax.random.split(key, 24)
    n_a = (DEPTH + 1) // 2
    n_b = DEPTH // 2

    def nrm(k, shape, scale):
        return jax.random.normal(k, shape, jnp.float32) * scale

    return {
        'x_prompt': nrm(ks[0], (BATCH, SEQ, D_MODEL), 1.0),
        'x_sample': nrm(ks[1], (DEC_BATCH, DEC_SEQ, D_MODEL), 1.0),
        'c_prompt': nrm(ks[2], (BATCH, D_MODEL), 1.0),
        'c_sample': nrm(ks[3], (DEC_BATCH, D_MODEL), 1.0),
        'norm_g': 1.0 + nrm(ks[4], (DEPTH, 2, D_MODEL), 0.05),
        'w_mod': nrm(ks[5], (DEPTH, D_MODEL, 6 * D_MODEL), 0.5 * D_MODEL ** -0.5),
        'b_mod': nrm(ks[6], (DEPTH, 6 * D_MODEL), 0.02),
        'a_w_qkv': nrm(ks[7], (n_a, D_MODEL, QKV_A), D_MODEL ** -0.5),
        'a_q_gain': 1.0 + nrm(ks[8], (n_a, HEAD_DIM), 0.05),
        'a_k_gain': 1.0 + nrm(ks[9], (n_a, HEAD_DIM), 0.05),
        'a_w_o': nrm(ks[10], (n_a, OUT_A, D_MODEL), OUT_A ** -0.5),
        'b_w_qkv': nrm(ks[11], (n_b, D_MODEL, QKV_B), D_MODEL ** -0.5),
        'b_q_gain': 1.0 + nrm(ks[12], (n_b, HEAD_DIM), 0.05),
        'b_k_gain': 1.0 + nrm(ks[13], (n_b, HEAD_DIM), 0.05),
        'b_rpb': nrm(ks[14], (n_b, N_HEADS_B, 2 * NA_ROWS - 1, 2 * NA_COLS - 1), 0.2),
        'b_w_o': nrm(ks[15], (n_b, OUT_B, D_MODEL), OUT_B ** -0.5),
        'w_router': nrm(ks[16], (DEPTH, D_MODEL, N_EXPERTS), D_MODEL ** -0.5),
        'b_router': nrm(ks[17], (DEPTH, N_EXPERTS), 0.01),
        'w_gu': nrm(ks[18], (DEPTH, N_EXPERTS, D_MODEL, 2 * D_FF), D_MODEL ** -0.5),
        'b_gu': nrm(ks[19], (DEPTH, N_EXPERTS, 2 * D_FF), 0.01),
        'w_dn': nrm(ks[20], (DEPTH, N_EXPERTS, D_FF, D_MODEL), D_FF ** -0.5),
        'b_dn': nrm(ks[21], (DEPTH, N_EXPERTS, D_MODEL), 0.01),
    }


def reference(x_prompt, x_sample, c_prompt, c_sample, norm_g, w_mod, b_mod,
              a_w_qkv, a_q_gain, a_k_gain, a_w_o, b_w_qkv, b_q_gain, b_k_gain, b_rpb, b_w_o,
              w_router, b_router, w_gu, b_gu, w_dn, b_dn):
    y_prompt = trunk(x_prompt, c_prompt, norm_g, w_mod, b_mod, a_w_qkv, a_q_gain, a_k_gain, a_w_o,
                     b_w_qkv, b_q_gain, b_k_gain, b_rpb, b_w_o,
                     w_router, b_router, w_gu, b_gu, w_dn, b_dn)
    y_sample = trunk(x_sample, c_sample, norm_g, w_mod, b_mod, a_w_qkv, a_q_gain, a_k_gain, a_w_o,
                     b_w_qkv, b_q_gain, b_k_gain, b_rpb, b_w_o,
                     w_router, b_router, w_gu, b_gu, w_dn, b_dn)
    return (y_prompt, y_sample)
```

```python
import functools

import numpy as np
import jax
import jax.numpy as jnp
from jax import lax
from jax.experimental import pallas as pl
from jax.experimental.pallas import tpu as pltpu

F32 = jnp.float32
BF16 = jnp.bfloat16

HEAD_DIM = 64
ROPE_THETA = 10000.0
NORM_EPS = 1e-6
NEG_INF = -1e30
DILATIONS = (1, 4, 16)
BAND_RADIUS = 64
HEADS_PER_GROUP_A = 8
N_HEADS_B = 16
GRID_W = 64
NA_ROWS = 8
NA_COLS = 16
N_EXPERTS = 32
TOP_K = 4
SWIGLU_LIMIT = 7.0
SWIGLU_ALPHA = 1.702

LANES = 128
VMEM_LIMIT_BYTES = 56 * 1024 * 1024

CHUNK = 2048
ROW_TILE = 512
BAND_TQ = 128
NA_TQ = 256
MOE_TILE = 256
ROUTER_PAD = LANES


def _cparams(sem):
    return pltpu.CompilerParams(dimension_semantics=sem, vmem_limit_bytes=VMEM_LIMIT_BYTES)


class _Tokens:
    def __init__(self, b_p, s_p, b_s, s_s):
        assert s_p % CHUNK == 0 and s_s % CHUNK == 0
        self.b_p, self.s_p, self.b_s, self.s_s = b_p, s_p, b_s, s_s
        self.n_p = b_p * s_p
        self.n = self.n_p + b_s * s_s
        self.n_chunks = self.n // CHUNK
        self.max_s = max(s_p, s_s)

    def batch_row(self, t0):
        return jnp.where(t0 < self.n_p, t0 // self.s_p, self.b_p + (t0 - self.n_p) // self.s_s)

    def pos(self, t0):
        return jnp.where(t0 < self.n_p, t0 % self.s_p, (t0 - self.n_p) % self.s_s)

    def seq_len(self, t0):
        return jnp.where(t0 < self.n_p, self.s_p, self.s_s)


def _mod_kernel(c_ref, w_ref, b_ref, o_ref):
    c = c_ref[...]
    act = c * jax.nn.sigmoid(c)
    o_ref[0] = jnp.dot(act, w_ref[0], precision=lax.Precision.HIGHEST,
                       preferred_element_type=F32) + b_ref[0]


def _modulation(c_all, w_mod, b_mod):
    depth, d, d6 = w_mod.shape
    rows = c_all.shape[0]
    return pl.pallas_call(
        _mod_kernel,
        grid=(depth, d6 // d),
        in_specs=[
            pl.BlockSpec((rows, d), lambda l, j: (0, 0)),
            pl.BlockSpec((1, d, d), lambda l, j: (l, 0, j)),
            pl.BlockSpec((1, 1, d), lambda l, j: (l, 0, j)),
        ],
        out_specs=pl.BlockSpec((1, rows, d), lambda l, j: (l, 0, j)),
        out_shape=jax.ShapeDtypeStruct((depth, rows, d6), F32),
        compiler_params=_cparams(("arbitrary", "arbitrary")),
        name="modulation",
    )(c_all, w_mod, b_mod.reshape(depth, 1, d6))


def _norm_modulate(x, g, shift, scale):
    y = x * lax.rsqrt(jnp.mean(x * x, axis=-1, keepdims=True) + NORM_EPS)
    return (y * g) * (1.0 + scale) + shift


def _qkv_kernel(x_ref, g_ref, sh_ref, sc_ref, w_ref, qg_ref, kg_ref, cos_ref, sin_ref, e_ref,
                o_ref, h_ref, *, kinds, rope):
    h_ref[...] = _norm_modulate(x_ref[...], g_ref[...], sh_ref[0], sc_ref[0]).astype(BF16)
    lane = lax.broadcasted_iota(jnp.int32, (1, LANES), 1)
    first_half = (lane % HEAD_DIM) < (HEAD_DIM // 2)
    for c in range(len(kinds) // 2):
        acc = jnp.dot(h_ref[...], w_ref[:, c * 2 * LANES:(c + 1) * 2 * LANES],
                      preferred_element_type=F32)
        for half in range(2):
            cb = 2 * c + half
            y = acc[:, half * LANES:(half + 1) * LANES]
            if kinds[cb] != "v":
                ss = jnp.dot((y * y).astype(BF16), e_ref[...], preferred_element_type=F32)
                gain = qg_ref[...] if kinds[cb] == "q" else kg_ref[...]
                y = y * lax.rsqrt(ss * (1.0 / HEAD_DIM) + NORM_EPS) * gain
                if rope:
                    partner = jnp.where(first_half, pltpu.roll(y, LANES - HEAD_DIM // 2, 1),
                                        pltpu.roll(y, HEAD_DIM // 2, 1))
                    y = y * cos_ref[...] + partner * sin_ref[...]
            o_ref[cb] = y.astype(BF16)


def _qkv_call(tk, x, g, mod3, shift_j, scale_j, w, q_gain, k_gain, cos, sin, kinds, rope):
    n, d = x.shape
    ncols = w.shape[1]
    ncb = ncols // LANES
    tm = ROW_TILE
    ones = np.kron(np.eye(LANES // HEAD_DIM), np.ones((HEAD_DIM, HEAD_DIM))).astype(np.float32)
    e = jnp.asarray(ones, BF16)
    scale = HEAD_DIM ** -0.5
    qg = jnp.tile(q_gain.reshape(1, HEAD_DIM), (1, LANES // HEAD_DIM)) * scale
    kg = jnp.tile(k_gain.reshape(1, HEAD_DIM), (1, LANES // HEAD_DIM))
    brow = lambda i: tk.batch_row(i * tm)
    prow = lambda i: tk.pos(i * tm) // tm
    kern = functools.partial(_qkv_kernel, kinds=kinds, rope=rope)
    return pl.pallas_call(
        kern,
        grid=(n // tm,),
        in_specs=[
            pl.BlockSpec((tm, d), lambda i: (i, 0)),
            pl.BlockSpec((1, d), lambda i: (0, 0)),
            pl.BlockSpec((1, 1, d), lambda i: (brow(i), 0, shift_j)),
            pl.BlockSpec((1, 1, d), lambda i: (brow(i), 0, scale_j)),
            pl.BlockSpec((d, ncols), lambda i: (0, 0)),
            pl.BlockSpec((1, LANES), lambda i: (0, 0)),
            pl.BlockSpec((1, LANES), lambda i: (0, 0)),
            pl.BlockSpec((tm, LANES), lambda i: (prow(i), 0)),
            pl.BlockSpec((tm, LANES), lambda i: (prow(i), 0)),
            pl.BlockSpec((LANES, LANES), lambda i: (0, 0)),
        ],
        out_specs=pl.BlockSpec((ncb, tm, LANES), lambda i: (0, i, 0)),
        out_shape=jax.ShapeDtypeStruct((ncb, n, LANES), BF16),
        scratch_shapes=[pltpu.VMEM((tm, d), BF16)],
        compiler_params=_cparams(("arbitrary",)),
        name="norm_qkv",
    )(x, g, mod3, mod3, w, qg, kg, cos, sin, e)


def _rope_tables(max_s):
    half = HEAD_DIM // 2
    inv = ROPE_THETA ** (-np.arange(half, dtype=np.float64) / half)
    ang = np.arange(max_s, dtype=np.float64)[:, None] * inv[None, :]
    cos = np.cos(ang)
    sin = np.sin(ang)
    cos_head = np.concatenate([cos, cos], axis=1)
    sin_head = np.concatenate([-sin, sin], axis=1)
    reps = LANES // HEAD_DIM
    return (jnp.asarray(np.tile(cos_head, (1, reps)), F32),
            jnp.asarray(np.tile(sin_head, (1, reps)), F32))


def _band_kernel(q_ref, kp_ref, kc_ref, kn_ref, vp_ref, vc_ref, vn_ref, o_ref, lse_ref, *,
                 lc, chunks_p, chunks_per_seq_p):
    c = pl.program_id(0)
    i = pl.program_id(2)
    tq = BAND_TQ
    r = BAND_RADIUS
    kw = tq + 2 * r
    in_prompt = c < chunks_p
    cofs = jnp.where(in_prompt, c % chunks_per_seq_p, 0)
    seq_rows = jnp.where(in_prompt, chunks_per_seq_p, 1) * lc
    u0 = cofs * lc + i * tq
    row = lax.broadcasted_iota(jnp.int32, (tq, kw), 0)
    col = lax.broadcasted_iota(jnp.int32, (tq, kw), 1)
    uk = u0 - r + col
    valid = (jnp.abs(col - r - row) <= r) & (uk >= 0) & (uk < seq_rows)
    lane = lax.broadcasted_iota(jnp.int32, (1, LANES), 1)
    lo = lane < HEAD_DIM
    for hp in range(q_ref.shape[0]):
        q = q_ref[hp, 0, 0]
        k = jnp.concatenate([kp_ref[hp, 0, 0], kc_ref[hp, 0, 0], kn_ref[hp, 0, 0]], axis=0)
        v = jnp.concatenate([vp_ref[hp, 0, 0], vc_ref[hp, 0, 0], vn_ref[hp, 0, 0]], axis=0)
        outs, lses = [], []
        for hh in range(2):
            sel = lo if hh == 0 else jnp.logical_not(lo)
            qm = jnp.where(sel, q, jnp.zeros_like(q))
            s = lax.dot_general(qm, k, (((1,), (1,)), ((), ())), preferred_element_type=F32)
            s = jnp.where(valid, s, NEG_INF)
            m = jnp.max(s, axis=-1, keepdims=True)
            p = jnp.exp(s - m)
            den = jnp.sum(p, axis=-1, keepdims=True)
            o = jnp.dot(p.astype(BF16), v, preferred_element_type=F32) / den
            outs.append(o)
            lses.append(m + jnp.log(den))
        o_ref[hp, 0, 0] = jnp.where(lo, outs[0], outs[1]).astype(BF16)
        lse_ref[hp, 0, 0] = jnp.where(lo, lses[0], lses[1])


def _band_call(tk, arr, q_blk, k_blk, v_blk, dil):
    nch = tk.n_chunks
    lc = CHUNK // dil
    tq = BAND_TQ
    r = BAND_RADIUS
    hpg = HEADS_PER_GROUP_A // 2
    per = tq // r
    nb = lc // r

    def cur(blk):
        return pl.BlockSpec((hpg, 1, 1, tq, LANES), lambda c, p, i: (blk, c, p, i, 0))

    def prev(blk):
        def imap(c, p, i):
            j = per * i - 1
            return (blk, jnp.where(j < 0, jnp.maximum(c - 1, 0), c), p, jnp.where(j < 0, nb - 1, j), 0)
        return pl.BlockSpec((hpg, 1, 1, r, LANES), imap)

    def nxt(blk):
        def imap(c, p, i):
            j = per * (i + 1)
            return (blk, jnp.where(j >= nb, jnp.minimum(c + 1, nch - 1), c), p,
                    jnp.where(j >= nb, 0, j), 0)
        return pl.BlockSpec((hpg, 1, 1, r, LANES), imap)

    kern = functools.partial(_band_kernel, lc=lc, chunks_p=tk.n_p // CHUNK,
                             chunks_per_seq_p=tk.s_p // CHUNK)
    out_spec = pl.BlockSpec((hpg, 1, 1, tq, LANES), lambda c, p, i: (0, c, p, i, 0))
    return pl.pallas_call(
        kern,
        grid=(nch, dil, lc // tq),
        in_specs=[cur(q_blk), prev(k_blk), cur(k_blk), nxt(k_blk), prev(v_blk), cur(v_blk), nxt(v_blk)],
        out_specs=[out_spec, out_spec],
        out_shape=[jax.ShapeDtypeStruct((hpg, nch, dil, lc, LANES), BF16),
                   jax.ShapeDtypeStruct((hpg, nch, dil, lc, LANES), F32)],
        compiler_params=_cparams(("arbitrary", "arbitrary", "arbitrary")),
        name="band_attention",
    )(arr, arr, arr, arr, arr, arr, arr)


def _dilated_attention(tk, qkv):
    n = tk.n
    nch = tk.n_chunks
    hpg = HEADS_PER_GROUP_A // 2
    ng = len(DILATIONS)
    outs = []
    for g, dil in enumerate(DILATIONS):
        lc = CHUNK // dil
        if dil == 1:
            arr = qkv.reshape(3 * ng * hpg, nch, 1, lc, LANES)
            blks = (g, ng + g, 2 * ng + g)
        else:
            arr = qkv.reshape(3, ng, hpg, nch, lc, dil, LANES)[:, g]
            arr = jnp.transpose(arr, (0, 1, 2, 4, 3, 5)).reshape(3 * hpg, nch, dil, lc, LANES)
            blks = (0, 1, 2)
        o, lse = _band_call(tk, arr, *blks, dil)
        if dil != 1:
            o = jnp.transpose(o, (0, 1, 3, 2, 4))
            lse = jnp.transpose(lse, (0, 1, 3, 2, 4))
        outs.append((o.reshape(hpg, n, LANES), lse.reshape(hpg, n, LANES)))
    return outs


def _na_kernel(q_ref, kp_ref, kc_ref, kn_ref, vp_ref, vc_ref, vn_ref, bias_ref, o_ref, kbuf, vbuf, *,
               rows_p, rows_s, blocks_p):
    bq = pl.program_id(1)
    tq = NA_TQ
    rows_per_blk = tq // GRID_W
    in_prompt = bq < blocks_p
    rows = jnp.where(in_prompt, rows_p, rows_s)
    blk_in_seq = jnp.where(in_prompt, bq % (rows_p // rows_per_blk),
                           (bq - blocks_p) % (rows_s // rows_per_blk))
    r0 = blk_in_seq * rows_per_blk
    kbuf[0 * tq:1 * tq] = kp_ref[0]
    kbuf[1 * tq:2 * tq] = kc_ref[0]
    kbuf[2 * tq:3 * tq] = kn_ref[0]
    vbuf[0 * tq:1 * tq] = vp_ref[0]
    vbuf[1 * tq:2 * tq] = vc_ref[0]
    vbuf[2 * tq:3 * tq] = vn_ref[0]
    lane = lax.broadcasted_iota(jnp.int32, (1, LANES), 1)
    lo = lane < HEAD_DIM
    nkeys = NA_ROWS * GRID_W
    for d in range(rows_per_blk):
        r = r0 + d
        rs = jnp.clip(r - NA_ROWS // 2, 0, rows - NA_ROWS)
        delta = r - rs
        off = pl.multiple_of((rs - (r0 - rows_per_blk)) * GRID_W, GRID_W)
        k = kbuf[pl.ds(off, nkeys), :]
        v = vbuf[pl.ds(off, nkeys), :]
        q = q_ref[0, d * GRID_W:(d + 1) * GRID_W, :]
        outs = []
        for hh in range(2):
            sel = lo if hh == 0 else jnp.logical_not(lo)
            qm = jnp.where(sel, q, jnp.zeros_like(q))
            s = lax.dot_general(qm, k, (((1,), (1,)), ((), ())), preferred_element_type=F32)
            s = s + bias_ref[0, hh, delta]
            m = jnp.max(s, axis=-1, keepdims=True)
            p = jnp.exp(s - m)
            den = jnp.sum(p, axis=-1, keepdims=True)
            outs.append(jnp.dot(p.astype(BF16), v, preferred_element_type=F32) / den)
        o_ref[0, d * GRID_W:(d + 1) * GRID_W, :] = jnp.where(lo, outs[0], outs[1]).astype(BF16)


def _na_bias_table(rpb):
    qc = np.arange(GRID_W)
    cstart = np.clip(qc - NA_COLS // 2, 0, GRID_W - NA_COLS)
    kc = np.arange(GRID_W)
    col_valid = (kc[None, :] >= cstart[:, None]) & (kc[None, :] < cstart[:, None] + NA_COLS)
    col_off = np.clip(kc[None, :] - qc[:, None] + NA_COLS - 1, 0, 2 * NA_COLS - 2)
    delta = np.arange(NA_ROWS)
    row_off = np.arange(NA_ROWS)[None, :] - delta[:, None] + NA_ROWS - 1
    t = rpb[:, row_off]
    t = t[:, :, :, col_off]
    t = jnp.where(jnp.asarray(col_valid)[None, None, None], t, NEG_INF)
    t = jnp.transpose(t, (0, 1, 3, 2, 4))
    h = rpb.shape[0]
    return t.reshape(h // 2, 2, NA_ROWS, GRID_W, NA_ROWS * GRID_W).astype(F32)


def _na_call(tk, qkv, bias):
    n = tk.n
    tq = NA_TQ
    nhp = N_HEADS_B // 2
    nblk = n // tq
    arr = qkv.reshape(3 * nhp, nblk, tq, LANES)
    nkeys = NA_ROWS * GRID_W

    arr3 = arr
    kern = functools.partial(_na_kernel, rows_p=tk.s_p // GRID_W, rows_s=tk.s_s // GRID_W,
                             blocks_p=tk.n_p // tq)

    def bspec(base, shift):
        return pl.BlockSpec((None, 1, tq, LANES),
                            lambda hp, b: (base + hp, jnp.clip(b + shift, 0, nblk - 1), 0, 0))

    return pl.pallas_call(
        kern,
        grid=(nhp, nblk),
        in_specs=[bspec(0, 0), bspec(nhp, -1), bspec(nhp, 0), bspec(nhp, 1),
                  bspec(2 * nhp, -1), bspec(2 * nhp, 0), bspec(2 * nhp, 1),
                  pl.BlockSpec((1, 2, NA_ROWS, GRID_W, nkeys), lambda hp, b: (hp, 0, 0, 0, 0))],
        out_specs=pl.BlockSpec((None, 1, tq, LANES), lambda hp, b: (hp, b, 0, 0)),
        out_shape=jax.ShapeDtypeStruct((nhp, nblk, tq, LANES), BF16),
        scratch_shapes=[pltpu.VMEM((3 * tq, LANES), BF16), pltpu.VMEM((3 * tq, LANES), BF16)],
        compiler_params=_cparams(("arbitrary", "arbitrary")),
        name="neighbourhood_attention",
    )(arr3, arr3, arr3, arr3, arr3, arr3, arr3, bias).reshape(nhp, n, LANES)


def _oproj_kernel(*refs, n_groups, n_tiles):
    o_refs = refs[:n_groups]
    l_refs = refs[n_groups:2 * n_groups] if n_groups > 1 else ()
    k = len(o_refs) + len(l_refs)
    (x_ref, wo_ref, g1_ref, ng_ref, sh_ref, sc_ref, wrh_ref, wrl_ref, br_ref,
     xo_ref, h_ref, lg_ref, m_ref) = refs[k:]
    for t in range(n_tiles):
        if n_groups == 1:
            m_ref[:, t * LANES:(t + 1) * LANES] = o_refs[0][t]
        else:
            ls = [l[t] for l in l_refs]
            mx = functools.reduce(jnp.maximum, ls)
            es = [jnp.exp(l - mx) for l in ls]
            den = functools.reduce(lambda a, b: a + b, es)
            num = functools.reduce(lambda a, b: a + b,
                                   [e * o[t].astype(F32) for e, o in zip(es, o_refs)])
            m_ref[:, t * LANES:(t + 1) * LANES] = (num / den).astype(BF16)
    mix = jnp.dot(m_ref[...], wo_ref[...], preferred_element_type=F32)
    x = x_ref[...] + g1_ref[0] * mix
    xo_ref[...] = x
    h = _norm_modulate(x, ng_ref[...], sh_ref[0], sc_ref[0])
    hi = h.astype(BF16)
    lo = (h - hi.astype(F32)).astype(BF16)
    h_ref[...] = hi
    lg = jnp.dot(hi, wrh_ref[...], preferred_element_type=F32)
    lg = lg + jnp.dot(lo, wrh_ref[...], preferred_element_type=F32)
    lg = lg + jnp.dot(hi, wrl_ref[...], preferred_element_type=F32)
    lg_ref[...] = lg + br_ref[...]


def _oproj_call(tk, os, ls, x, w_o, mod3, g1_j, ng, sh_j, sc_j, w_router, b_router):
    n, d = x.shape
    tm = ROW_TILE
    n_groups = len(os)
    n_tiles = os[0].shape[0]
    k_in = n_tiles * LANES
    ne = w_router.shape[1]
    wr = jnp.zeros((d, ROUTER_PAD), F32).at[:, :ne].set(w_router)
    wr_hi = wr.astype(BF16)
    wr_lo = (wr - wr_hi.astype(F32)).astype(BF16)
    br = jnp.zeros((1, ROUTER_PAD), F32).at[0, :ne].set(b_router)
    brow = lambda i: tk.batch_row(i * tm)
    tile_spec = pl.BlockSpec((n_tiles, tm, LANES), lambda i: (0, i, 0))
    row_spec = pl.BlockSpec((tm, d), lambda i: (i, 0))
    mod_spec = lambda j: pl.BlockSpec((1, 1, d), lambda i: (brow(i), 0, j))
    kern = functools.partial(_oproj_kernel, n_groups=n_groups, n_tiles=n_tiles)
    return pl.pallas_call(
        kern,
        grid=(n // tm,),
        in_specs=[tile_spec] * (n_groups + len(ls)) + [
            row_spec,
            pl.BlockSpec((k_in, d), lambda i: (0, 0)),
            mod_spec(g1_j),
            pl.BlockSpec((1, d), lambda i: (0, 0)),
            mod_spec(sh_j),
            mod_spec(sc_j),
            pl.BlockSpec((d, ROUTER_PAD), lambda i: (0, 0)),
            pl.BlockSpec((d, ROUTER_PAD), lambda i: (0, 0)),
            pl.BlockSpec((1, ROUTER_PAD), lambda i: (0, 0)),
        ],
        out_specs=[row_spec, row_spec, pl.BlockSpec((tm, ROUTER_PAD), lambda i: (i, 0))],
        out_shape=[jax.ShapeDtypeStruct((n, d), F32), jax.ShapeDtypeStruct((n, d), BF16),
                   jax.ShapeDtypeStruct((n, ROUTER_PAD), F32)],
        scratch_shapes=[pltpu.VMEM((tm, k_in), BF16)],
        compiler_params=_cparams(("arbitrary",)),
        name="oproj_norm_router",
    )(*os, *ls, x, w_o, mod3, ng, mod3, mod3, wr_hi, wr_lo, br)


def _moe_kernel(te_ref, nu_ref, x_ref, wgu_ref, bgu_ref, wdn_ref, bdn_ref, o_ref, acc_ref, *, d_ff, fchunk):
    t = pl.program_id(0)

    @pl.when(t < nu_ref[0])
    def _():
        x = x_ref[...]
        for c in range(d_ff // fchunk):
            lo, hi = c * fchunk, (c + 1) * fchunk
            gate = jnp.dot(x, wgu_ref[0, :, lo:hi], preferred_element_type=F32) + bgu_ref[0, :, lo:hi]
            up = jnp.dot(x, wgu_ref[0, :, d_ff + lo:d_ff + hi],
                         preferred_element_type=F32) + bgu_ref[0, :, d_ff + lo:d_ff + hi]
            gate = jnp.minimum(gate, SWIGLU_LIMIT)
            up = jnp.clip(up, -SWIGLU_LIMIT, SWIGLU_LIMIT)
            a = ((up + 1.0) * gate * jax.nn.sigmoid(SWIGLU_ALPHA * gate)).astype(BF16)
            part = jnp.dot(a, wdn_ref[0, lo:hi, :], preferred_element_type=F32)
            if c == 0:
                acc_ref[...] = part + bdn_ref[0]
            else:
                acc_ref[...] += part
        o_ref[...] = acc_ref[...].astype(BF16)

    @pl.when(t >= nu_ref[0])
    def _():
        o_ref[...] = jnp.zeros_like(o_ref)


def _moe_call(xs, tile_expert, n_used, w_gu, b_gu, w_dn, b_dn):
    cap, d = xs.shape
    ne, _, d_ff2 = w_gu.shape
    d_ff = d_ff2 // 2
    tm = MOE_TILE
    n_tiles = cap // tm
    kern = functools.partial(_moe_kernel, d_ff=d_ff, fchunk=256)
    grid_spec = pltpu.PrefetchScalarGridSpec(
        num_scalar_prefetch=2,
        grid=(n_tiles,),
        in_specs=[
            pl.BlockSpec((tm, d), lambda t, te, nu: (t, 0)),
            pl.BlockSpec((1, d, d_ff2), lambda t, te, nu: (te[t], 0, 0)),
            pl.BlockSpec((1, 1, d_ff2), lambda t, te, nu: (te[t], 0, 0)),
            pl.BlockSpec((1, d_ff, d), lambda t, te, nu: (te[t], 0, 0)),
            pl.BlockSpec((1, 1, d), lambda t, te, nu: (te[t], 0, 0)),
        ],
        out_specs=pl.BlockSpec((tm, d), lambda t, te, nu: (t, 0)),
        scratch_shapes=[pltpu.VMEM((tm, d), F32)],
    )
    return pl.pallas_call(
        kern,
        grid_spec=grid_spec,
        out_shape=jax.ShapeDtypeStruct((cap, d), BF16),
        compiler_params=_cparams(("arbitrary",)),
        name="moe_experts",
    )(tile_expert, n_used, xs, w_gu, b_gu.reshape(ne, 1, d_ff2), w_dn, b_dn.reshape(ne, 1, d))


def _route(logits, n):
    ne = N_EXPERTS
    tm = MOE_TILE
    top_val, top_idx = lax.top_k(logits[:, :ne], TOP_K)
    gates = jax.nn.softmax(top_val, axis=-1)
    e_flat = top_idx.reshape(-1).astype(jnp.int32)
    nk = n * TOP_K
    onehot = (e_flat[:, None] == jnp.arange(ne, dtype=jnp.int32)[None, :]).astype(jnp.int32)
    csum = jnp.cumsum(onehot, axis=0)
    rank = jnp.take_along_axis(csum, e_flat[:, None], axis=1)[:, 0] - 1
    counts = csum[-1]
    starts = jnp.cumsum(counts) - counts
    padded = (counts + tm - 1) // tm * tm
    pad_ends = jnp.cumsum(padded)
    pad_starts = pad_ends - padded
    pos = (pad_starts[e_flat] + rank).reshape(n, TOP_K)
    n_tiles = nk // tm + ne
    cap = n_tiles * tm
    tile_expert = jnp.minimum(
        jnp.searchsorted(pad_ends, jnp.arange(n_tiles, dtype=jnp.int32) * tm, side="right"),
        ne - 1).astype(jnp.int32)
    n_used = (pad_ends[-1] // tm).astype(jnp.int32).reshape(1)
    idx_bits = int(np.ceil(np.log2(nk)))
    keys = (e_flat << idx_bits) + jnp.arange(nk, dtype=jnp.int32)
    tok_sorted = (jnp.sort(keys) & ((1 << idx_bits) - 1)) // TOP_K
    slot = jnp.arange(cap, dtype=jnp.int32)
    e_s = tile_expert[slot // tm]
    off = slot - pad_starts[e_s]
    valid = off < counts[e_s]
    j = jnp.clip(starts[e_s] + off, 0, nk - 1)
    slot_tok = jnp.where(valid, tok_sorted[j], n)
    return gates, pos, slot_tok, tile_expert, n_used


def _moe(tk, hff, logits, x, gate2, w_gu, b_gu, w_dn, b_dn):
    n, d = hff.shape
    gates, pos, slot_tok, tile_expert, n_used = _route(logits, n)
    h_pad = jnp.concatenate([hff, jnp.zeros((1, d), hff.dtype)], axis=0)
    xs = h_pad[slot_tok]
    yb = _moe_call(xs, tile_expert, n_used, w_gu, b_gu, w_dn, b_dn)
    y = jnp.sum(yb[pos].astype(F32) * gates[:, :, None], axis=1)
    return x + gate2 * y


def _per_token(tk, v):
    vp = jnp.repeat(v[:tk.b_p], tk.s_p, axis=0)
    vs = jnp.repeat(v[tk.b_p:tk.b_p + tk.b_s], tk.s_s, axis=0)
    return jnp.concatenate([vp, vs], axis=0)


@jax.jit
def _forward(x_prompt, x_sample, c_prompt, c_sample, norm_g, w_mod, b_mod, a_w_qkv, a_q_gain,
             a_k_gain, a_w_o, b_w_qkv, b_q_gain, b_k_gain, b_rpb, b_w_o, w_router, b_router,
             w_gu, b_gu, w_dn, b_dn):
    b_p, s_p, d = x_prompt.shape
    b_s, s_s, _ = x_sample.shape
    tk = _Tokens(b_p, s_p, b_s, s_s)
    depth = w_mod.shape[0]
    x = jnp.concatenate([x_prompt.reshape(-1, d), x_sample.reshape(-1, d)], axis=0)

    nb = b_p + b_s
    rows = -(-nb // 8) * 8
    c_all = jnp.zeros((rows, d), F32).at[:nb].set(jnp.concatenate([c_prompt, c_sample], axis=0))
    mods = _modulation(c_all, w_mod, b_mod)

    cos, sin = _rope_tables(tk.max_s)
    kinds_a = tuple("qkv"[cb // (len(DILATIONS) * HEADS_PER_GROUP_A // 2)]
                    for cb in range(3 * len(DILATIONS) * HEADS_PER_GROUP_A // 2))
    kinds_b = tuple("qkv"[cb // (N_HEADS_B // 2)] for cb in range(3 * N_HEADS_B // 2))
    w_gu_b = w_gu.astype(BF16)
    w_dn_b = w_dn.astype(BF16)

    for layer in range(depth):
        mod3 = mods[layer].reshape(rows, 1, 6 * d)
        j = layer // 2
        if layer % 2 == 0:
            qkv = _qkv_call(tk, x, norm_g[layer, 0].reshape(1, d), mod3, 0, 1,
                            a_w_qkv[j].astype(BF16), a_q_gain[j], a_k_gain[j], cos, sin, kinds_a, True)
            groups = _dilated_attention(tk, qkv)
            os = [g[0] for g in groups]
            ls = [g[1] for g in groups]
            w_o = a_w_o[j]
        else:
            qkv = _qkv_call(tk, x, norm_g[layer, 0].reshape(1, d), mod3, 0, 1,
                            b_w_qkv[j].astype(BF16), b_q_gain[j], b_k_gain[j], cos, sin, kinds_b, False)
            os = [_na_call(tk, qkv, _na_bias_table(b_rpb[j]))]
            ls = []
            w_o = b_w_o[j]
        x, hff, logits = _oproj_call(tk, os, ls, x, w_o.astype(BF16), mod3, 2,
                                     norm_g[layer, 1].reshape(1, d), 3, 4,
                                     w_router[layer], b_router[layer])
        gate2 = _per_token(tk, mods[layer][:, 5 * d:6 * d])
        x = _moe(tk, hff, logits, x, gate2, w_gu_b[layer], b_gu[layer], w_dn_b[layer], b_dn[layer])

    y_prompt = x[:tk.n_p].reshape(b_p, s_p, d)
    y_sample = x[tk.n_p:].reshape(b_s, s_s, d)
    return y_prompt, y_sample


def kernel(x_prompt, x_sample, c_prompt, c_sample, norm_g, w_mod, b_mod, a_w_qkv, a_q_gain, a_k_gain,
           a_w_o, b_w_qkv, b_q_gain, b_k_gain, b_rpb, b_w_o, w_router, b_router, w_gu, b_gu, w_dn, b_dn):
    return _forward(x_prompt, x_sample, c_prompt, c_sample, norm_g, w_mod, b_mod, a_w_qkv, a_q_gain,
                    a_k_gain, a_w_o, b_w_qkv, b_q_gain, b_k_gain, b_rpb, b_w_o, w_router, b_router,
                    w_gu, b_gu, w_dn, b_dn)
```

```python
import functools

import numpy as np
import jax
import jax.numpy as jnp
from jax import lax
from jax.experimental import pallas as pl
from jax.experimental.pallas import tpu as pltpu

F32 = jnp.float32
BF16 = jnp.bfloat16

HEAD_DIM = 64
ROPE_THETA = 10000.0
NORM_EPS = 1e-6
NEG_INF = -1e30
DILATIONS = (1, 4, 16)
BAND_RADIUS = 64
HEADS_PER_GROUP_A = 8
N_HEADS_B = 16
GRID_W = 64
NA_ROWS = 8
NA_COLS = 16
N_EXPERTS = 32
TOP_K = 4
SWIGLU_LIMIT = 7.0
SWIGLU_ALPHA = 1.702

LANES = 128
VMEM_LIMIT_BYTES = 56 * 1024 * 1024

CHUNK = 2048
ROW_TILE = 512
BAND_TQ = 128
NA_TQ = 512
MOE_TILE = 512
COMBINE_TILE = 256
ROUTER_PAD = LANES


def _cparams(sem):
    return pltpu.CompilerParams(dimension_semantics=sem, vmem_limit_bytes=VMEM_LIMIT_BYTES)


class _Tokens:
    def __init__(self, b_p, s_p, b_s, s_s):
        assert s_p % CHUNK == 0 and s_s % CHUNK == 0
        self.b_p, self.s_p, self.b_s, self.s_s = b_p, s_p, b_s, s_s
        self.n_p = b_p * s_p
        self.n = self.n_p + b_s * s_s
        self.n_chunks = self.n // CHUNK
        self.max_s = max(s_p, s_s)

    def batch_row(self, t0):
        return jnp.where(t0 < self.n_p, t0 // self.s_p, self.b_p + (t0 - self.n_p) // self.s_s)

    def pos(self, t0):
        return jnp.where(t0 < self.n_p, t0 % self.s_p, (t0 - self.n_p) % self.s_s)

    def seq_len(self, t0):
        return jnp.where(t0 < self.n_p, self.s_p, self.s_s)


def _mod_kernel(c_ref, w_ref, b_ref, o_ref):
    c = c_ref[...]
    act = c * jax.nn.sigmoid(c)
    o_ref[0] = jnp.dot(act, w_ref[0], precision=lax.Precision.HIGHEST,
                       preferred_element_type=F32) + b_ref[0]


def _modulation(c_all, w_mod, b_mod):
    depth, d, d6 = w_mod.shape
    rows = c_all.shape[0]
    return pl.pallas_call(
        _mod_kernel,
        grid=(depth, d6 // d),
        in_specs=[
            pl.BlockSpec((rows, d), lambda l, j: (0, 0)),
            pl.BlockSpec((1, d, d), lambda l, j: (l, 0, j)),
            pl.BlockSpec((1, 1, d), lambda l, j: (l, 0, j)),
        ],
        out_specs=pl.BlockSpec((1, rows, d), lambda l, j: (l, 0, j)),
        out_shape=jax.ShapeDtypeStruct((depth, rows, d6), F32),
        compiler_params=_cparams(("arbitrary", "arbitrary")),
        name="modulation",
    )(c_all, w_mod, b_mod.reshape(depth, 1, d6))


def _norm_modulate(x, g, shift, scale):
    y = x * lax.rsqrt(jnp.mean(x * x, axis=-1, keepdims=True) + NORM_EPS)
    return (y * g) * (1.0 + scale) + shift


def _qkv_kernel(x_ref, g_ref, sh_ref, sc_ref, w_ref, qg_ref, kg_ref, cos_ref, sin_ref, e_ref,
                o_ref, h_ref, *, kinds, rope):
    h_ref[...] = _norm_modulate(x_ref[...], g_ref[...], sh_ref[0], sc_ref[0]).astype(BF16)
    lane = lax.broadcasted_iota(jnp.int32, (1, LANES), 1)
    first_half = (lane % HEAD_DIM) < (HEAD_DIM // 2)
    for c in range(len(kinds) // 2):
        acc = jnp.dot(h_ref[...], w_ref[:, c * 2 * LANES:(c + 1) * 2 * LANES],
                      preferred_element_type=F32)
        for half in range(2):
            cb = 2 * c + half
            y = acc[:, half * LANES:(half + 1) * LANES]
            if kinds[cb] != "v":
                ss = jnp.dot((y * y).astype(BF16), e_ref[...], preferred_element_type=F32)
                gain = qg_ref[...] if kinds[cb] == "q" else kg_ref[...]
                y = y * lax.rsqrt(ss * (1.0 / HEAD_DIM) + NORM_EPS) * gain
                if rope:
                    partner = jnp.where(first_half, pltpu.roll(y, LANES - HEAD_DIM // 2, 1),
                                        pltpu.roll(y, HEAD_DIM // 2, 1))
                    y = y * cos_ref[...] + partner * sin_ref[...]
            o_ref[cb] = y.astype(BF16)


def _qkv_call(tk, x, g, mod3, shift_j, scale_j, w, q_gain, k_gain, cos, sin, kinds, rope):
    n, d = x.shape
    ncols = w.shape[1]
    ncb = ncols // LANES
    tm = ROW_TILE
    ones = np.kron(np.eye(LANES // HEAD_DIM), np.ones((HEAD_DIM, HEAD_DIM))).astype(np.float32)
    e = jnp.asarray(ones, BF16)
    scale = HEAD_DIM ** -0.5
    qg = jnp.tile(q_gain.reshape(1, HEAD_DIM), (1, LANES // HEAD_DIM)) * scale
    kg = jnp.tile(k_gain.reshape(1, HEAD_DIM), (1, LANES // HEAD_DIM))
    brow = lambda i: tk.batch_row(i * tm)
    prow = lambda i: tk.pos(i * tm) // tm
    kern = functools.partial(_qkv_kernel, kinds=kinds, rope=rope)
    return pl.pallas_call(
        kern,
        grid=(n // tm,),
        in_specs=[
            pl.BlockSpec((tm, d), lambda i: (i, 0)),
            pl.BlockSpec((1, d), lambda i: (0, 0)),
            pl.BlockSpec((1, 1, d), lambda i: (brow(i), 0, shift_j)),
            pl.BlockSpec((1, 1, d), lambda i: (brow(i), 0, scale_j)),
            pl.BlockSpec((d, ncols), lambda i: (0, 0)),
            pl.BlockSpec((1, LANES), lambda i: (0, 0)),
            pl.BlockSpec((1, LANES), lambda i: (0, 0)),
            pl.BlockSpec((tm, LANES), lambda i: (prow(i), 0)),
            pl.BlockSpec((tm, LANES), lambda i: (prow(i), 0)),
            pl.BlockSpec((LANES, LANES), lambda i: (0, 0)),
        ],
        out_specs=pl.BlockSpec((ncb, tm, LANES), lambda i: (0, i, 0)),
        out_shape=jax.ShapeDtypeStruct((ncb, n, LANES), BF16),
        scratch_shapes=[pltpu.VMEM((tm, d), BF16)],
        compiler_params=_cparams(("arbitrary",)),
        name="norm_qkv",
    )(x, g, mod3, mod3, w, qg, kg, cos, sin, e)


def _rope_tables(max_s):
    half = HEAD_DIM // 2
    inv = ROPE_THETA ** (-np.arange(half, dtype=np.float64) / half)
    ang = np.arange(max_s, dtype=np.float64)[:, None] * inv[None, :]
    cos = np.cos(ang)
    sin = np.sin(ang)
    cos_head = np.concatenate([cos, cos], axis=1)
    sin_head = np.concatenate([-sin, sin], axis=1)
    reps = LANES // HEAD_DIM
    return (jnp.asarray(np.tile(cos_head, (1, reps)), F32),
            jnp.asarray(np.tile(sin_head, (1, reps)), F32))


def _band_kernel(q_ref, kp_ref, kc_ref, kn_ref, vp_ref, vc_ref, vn_ref, o_ref, lse_ref, *,
                 lc, chunks_p, chunks_per_seq_p):
    c = pl.program_id(0)
    i = pl.program_id(2)
    tq = BAND_TQ
    r = BAND_RADIUS
    kw = tq + 2 * r
    in_prompt = c < chunks_p
    cofs = jnp.where(in_prompt, c % chunks_per_seq_p, 0)
    seq_rows = jnp.where(in_prompt, chunks_per_seq_p, 1) * lc
    u0 = cofs * lc + i * tq
    row = lax.broadcasted_iota(jnp.int32, (tq, kw), 0)
    col = lax.broadcasted_iota(jnp.int32, (tq, kw), 1)
    uk = u0 - r + col
    valid = (jnp.abs(col - r - row) <= r) & (uk >= 0) & (uk < seq_rows)
    lane = lax.broadcasted_iota(jnp.int32, (1, LANES), 1)
    lo = lane < HEAD_DIM
    hi = jnp.logical_not(lo)
    nhp = q_ref.shape[0]
    qs, ks, vs = [], [], []
    for hp in range(nhp):
        q = q_ref[hp, 0, 0]
        k = jnp.concatenate([kp_ref[hp, 0, 0], kc_ref[hp, 0, 0], kn_ref[hp, 0, 0]], axis=0)
        v = jnp.concatenate([vp_ref[hp, 0, 0], vc_ref[hp, 0, 0], vn_ref[hp, 0, 0]], axis=0)
        qs += [jnp.where(lo, q, jnp.zeros_like(q)), jnp.where(hi, q, jnp.zeros_like(q))]
        ks += [k, k]
        vs += [v, v]
    s = lax.dot_general(jnp.stack(qs), jnp.stack(ks), (((2,), (2,)), ((0,), (0,))),
                        preferred_element_type=F32)
    s = jnp.where(valid[None], s, NEG_INF)
    m = jnp.max(s, axis=-1, keepdims=True)
    p = jnp.exp(s - m)
    den = jnp.sum(p, axis=-1, keepdims=True)
    o = lax.dot_general(p.astype(BF16), jnp.stack(vs), (((2,), (1,)), ((0,), (0,))),
                        preferred_element_type=F32) / den
    lse = m + jnp.log(den)
    for hp in range(nhp):
        o_ref[hp, 0, 0] = jnp.where(lo, o[2 * hp], o[2 * hp + 1]).astype(BF16)
        lse_ref[hp, 0, 0] = jnp.where(lo, lse[2 * hp], lse[2 * hp + 1])


def _band_call(tk, arr, q_blk, k_blk, v_blk, dil):
    nch = tk.n_chunks
    lc = CHUNK // dil
    tq = BAND_TQ
    r = BAND_RADIUS
    hpg = HEADS_PER_GROUP_A // 2
    per = tq // r
    nb = lc // r

    def cur(blk):
        return pl.BlockSpec((hpg, 1, 1, tq, LANES), lambda c, p, i: (blk, c, p, i, 0))

    def prev(blk):
        def imap(c, p, i):
            j = per * i - 1
            return (blk, jnp.where(j < 0, jnp.maximum(c - 1, 0), c), p, jnp.where(j < 0, nb - 1, j), 0)
        return pl.BlockSpec((hpg, 1, 1, r, LANES), imap)

    def nxt(blk):
        def imap(c, p, i):
            j = per * (i + 1)
            return (blk, jnp.where(j >= nb, jnp.minimum(c + 1, nch - 1), c), p,
                    jnp.where(j >= nb, 0, j), 0)
        return pl.BlockSpec((hpg, 1, 1, r, LANES), imap)

    kern = functools.partial(_band_kernel, lc=lc, chunks_p=tk.n_p // CHUNK,
                             chunks_per_seq_p=tk.s_p // CHUNK)
    out_spec = pl.BlockSpec((hpg, 1, 1, tq, LANES), lambda c, p, i: (0, c, p, i, 0))
    return pl.pallas_call(
        kern,
        grid=(nch, dil, lc // tq),
        in_specs=[cur(q_blk), prev(k_blk), cur(k_blk), nxt(k_blk), prev(v_blk), cur(v_blk), nxt(v_blk)],
        out_specs=[out_spec, out_spec],
        out_shape=[jax.ShapeDtypeStruct((hpg, nch, dil, lc, LANES), BF16),
                   jax.ShapeDtypeStruct((hpg, nch, dil, lc, LANES), F32)],
        compiler_params=_cparams(("arbitrary", "arbitrary", "arbitrary")),
        name="band_attention",
    )(arr, arr, arr, arr, arr, arr, arr)


def _dilated_attention(tk, qkv):
    n = tk.n
    nch = tk.n_chunks
    hpg = HEADS_PER_GROUP_A // 2
    ng = len(DILATIONS)
    outs = []
    for g, dil in enumerate(DILATIONS):
        lc = CHUNK // dil
        if dil == 1:
            arr = qkv.reshape(3 * ng * hpg, nch, 1, lc, LANES)
            blks = (g, ng + g, 2 * ng + g)
        else:
            arr = qkv.reshape(3, ng, hpg, nch, lc, dil, LANES)[:, g]
            arr = jnp.transpose(arr, (0, 1, 2, 4, 3, 5)).reshape(3 * hpg, nch, dil, lc, LANES)
            blks = (0, 1, 2)
        o, lse = _band_call(tk, arr, *blks, dil)
        if dil != 1:
            o = jnp.transpose(o, (0, 1, 3, 2, 4))
            lse = jnp.transpose(lse, (0, 1, 3, 2, 4))
        outs.append((o.reshape(hpg, n, LANES), lse.reshape(hpg, n, LANES)))
    return outs


def _na_kernel(q_ref, kp_ref, kc_ref, kn_ref, vp_ref, vc_ref, vn_ref, bias_ref, o_ref, kbuf, vbuf, *,
               rows_p, rows_s, blocks_p):
    bq = pl.program_id(1)
    tq = NA_TQ
    rows_per_blk = tq // GRID_W
    in_prompt = bq < blocks_p
    rows = jnp.where(in_prompt, rows_p, rows_s)
    blk_in_seq = jnp.where(in_prompt, bq % (rows_p // rows_per_blk),
                           (bq - blocks_p) % (rows_s // rows_per_blk))
    r0 = blk_in_seq * rows_per_blk
    kbuf[0 * tq:1 * tq] = kp_ref[0]
    kbuf[1 * tq:2 * tq] = kc_ref[0]
    kbuf[2 * tq:3 * tq] = kn_ref[0]
    vbuf[0 * tq:1 * tq] = vp_ref[0]
    vbuf[1 * tq:2 * tq] = vc_ref[0]
    vbuf[2 * tq:3 * tq] = vn_ref[0]
    lane = lax.broadcasted_iota(jnp.int32, (1, LANES), 1)
    lo = lane < HEAD_DIM
    hi = jnp.logical_not(lo)
    nkeys = NA_ROWS * GRID_W
    qs, ks, vs, bs = [], [], [], []
    for d in range(rows_per_blk):
        r = r0 + d
        rs = jnp.clip(r - NA_ROWS // 2, 0, rows - NA_ROWS)
        delta = r - rs
        off = pl.multiple_of((rs - (r0 - rows_per_blk)) * GRID_W, GRID_W)
        k = kbuf[pl.ds(off, nkeys), :]
        v = vbuf[pl.ds(off, nkeys), :]
        q = q_ref[0, d * GRID_W:(d + 1) * GRID_W, :]
        qs += [jnp.where(lo, q, jnp.zeros_like(q)), jnp.where(hi, q, jnp.zeros_like(q))]
        ks += [k, k]
        vs += [v, v]
        bs += [bias_ref[0, 0, delta], bias_ref[0, 1, delta]]
    s = lax.dot_general(jnp.stack(qs), jnp.stack(ks), (((2,), (2,)), ((0,), (0,))),
                        preferred_element_type=F32)
    s = s + jnp.stack(bs)
    m = jnp.max(s, axis=-1, keepdims=True)
    p = jnp.exp(s - m)
    den = jnp.sum(p, axis=-1, keepdims=True)
    o = lax.dot_general(p.astype(BF16), jnp.stack(vs), (((2,), (1,)), ((0,), (0,))),
                        preferred_element_type=F32) / den
    for d in range(rows_per_blk):
        o_ref[0, d * GRID_W:(d + 1) * GRID_W, :] = jnp.where(lo, o[2 * d], o[2 * d + 1]).astype(BF16)


def _na_bias_table(rpb):
    qc = np.arange(GRID_W)
    cstart = np.clip(qc - NA_COLS // 2, 0, GRID_W - NA_COLS)
    kc = np.arange(GRID_W)
    col_valid = (kc[None, :] >= cstart[:, None]) & (kc[None, :] < cstart[:, None] + NA_COLS)
    col_off = np.clip(kc[None, :] - qc[:, None] + NA_COLS - 1, 0, 2 * NA_COLS - 2)
    delta = np.arange(NA_ROWS)
    row_off = np.arange(NA_ROWS)[None, :] - delta[:, None] + NA_ROWS - 1
    t = rpb[:, row_off]
    t = t[:, :, :, col_off]
    t = jnp.where(jnp.asarray(col_valid)[None, None, None], t, NEG_INF)
    t = jnp.transpose(t, (0, 1, 3, 2, 4))
    h = rpb.shape[0]
    return t.reshape(h // 2, 2, NA_ROWS, GRID_W, NA_ROWS * GRID_W).astype(F32)


def _na_call(tk, qkv, bias):
    n = tk.n
    tq = NA_TQ
    nhp = N_HEADS_B // 2
    nblk = n // tq
    arr = qkv.reshape(3 * nhp, nblk, tq, LANES)
    nkeys = NA_ROWS * GRID_W

    arr3 = arr
    kern = functools.partial(_na_kernel, rows_p=tk.s_p // GRID_W, rows_s=tk.s_s // GRID_W,
                             blocks_p=tk.n_p // tq)

    def bspec(base, shift):
        return pl.BlockSpec((None, 1, tq, LANES),
                            lambda hp, b: (base + hp, jnp.clip(b + shift, 0, nblk - 1), 0, 0))

    return pl.pallas_call(
        kern,
        grid=(nhp, nblk),
        in_specs=[bspec(0, 0), bspec(nhp, -1), bspec(nhp, 0), bspec(nhp, 1),
                  bspec(2 * nhp, -1), bspec(2 * nhp, 0), bspec(2 * nhp, 1),
                  pl.BlockSpec((1, 2, NA_ROWS, GRID_W, nkeys), lambda hp, b: (hp, 0, 0, 0, 0))],
        out_specs=pl.BlockSpec((None, 1, tq, LANES), lambda hp, b: (hp, b, 0, 0)),
        out_shape=jax.ShapeDtypeStruct((nhp, nblk, tq, LANES), BF16),
        scratch_shapes=[pltpu.VMEM((3 * tq, LANES), BF16), pltpu.VMEM((3 * tq, LANES), BF16)],
        compiler_params=_cparams(("arbitrary", "arbitrary")),
        name="neighbourhood_attention",
    )(arr3, arr3, arr3, arr3, arr3, arr3, arr3, bias).reshape(nhp, n, LANES)


def _oproj_kernel(*refs, n_groups, n_tiles):
    o_refs = refs[:n_groups]
    l_refs = refs[n_groups:2 * n_groups] if n_groups > 1 else ()
    k = len(o_refs) + len(l_refs)
    (x_ref, wo_ref, g1_ref, ng_ref, sh_ref, sc_ref, wrh_ref, wrl_ref, br_ref,
     xo_ref, h_ref, lg_ref, m_ref) = refs[k:]
    for t in range(n_tiles):
        if n_groups == 1:
            m_ref[:, t * LANES:(t + 1) * LANES] = o_refs[0][t]
        else:
            ls = [l[t] for l in l_refs]
            mx = functools.reduce(jnp.maximum, ls)
            es = [jnp.exp(l - mx) for l in ls]
            den = functools.reduce(lambda a, b: a + b, es)
            num = functools.reduce(lambda a, b: a + b,
                                   [e * o[t].astype(F32) for e, o in zip(es, o_refs)])
            m_ref[:, t * LANES:(t + 1) * LANES] = (num / den).astype(BF16)
    mix = jnp.dot(m_ref[...], wo_ref[...], preferred_element_type=F32)
    x = x_ref[...] + g1_ref[0] * mix
    xo_ref[...] = x
    h = _norm_modulate(x, ng_ref[...], sh_ref[0], sc_ref[0])
    hi = h.astype(BF16)
    lo = (h - hi.astype(F32)).astype(BF16)
    tm, d = h.shape
    nsub = d // LANES
    for j in range(nsub):
        h_ref[pl.ds(j, tm, stride=nsub), :] = h[:, j * LANES:(j + 1) * LANES]
    lg = jnp.dot(hi, wrh_ref[...], preferred_element_type=F32)
    lg = lg + jnp.dot(lo, wrh_ref[...], preferred_element_type=F32)
    lg = lg + jnp.dot(hi, wrl_ref[...], preferred_element_type=F32)
    lg_ref[...] = lg + br_ref[...]


def _oproj_call(tk, os, ls, x, w_o, mod3, g1_j, ng, sh_j, sc_j, w_router, b_router):
    n, d = x.shape
    tm = ROW_TILE
    n_groups = len(os)
    n_tiles = os[0].shape[0]
    k_in = n_tiles * LANES
    ne = w_router.shape[1]
    wr = jnp.zeros((d, ROUTER_PAD), F32).at[:, :ne].set(w_router)
    wr_hi = wr.astype(BF16)
    wr_lo = (wr - wr_hi.astype(F32)).astype(BF16)
    br = jnp.zeros((1, ROUTER_PAD), F32).at[0, :ne].set(b_router)
    brow = lambda i: tk.batch_row(i * tm)
    tile_spec = pl.BlockSpec((n_tiles, tm, LANES), lambda i: (0, i, 0))
    row_spec = pl.BlockSpec((tm, d), lambda i: (i, 0))
    mod_spec = lambda j: pl.BlockSpec((1, 1, d), lambda i: (brow(i), 0, j))
    kern = functools.partial(_oproj_kernel, n_groups=n_groups, n_tiles=n_tiles)
    return pl.pallas_call(
        kern,
        grid=(n // tm,),
        in_specs=[tile_spec] * (n_groups + len(ls)) + [
            row_spec,
            pl.BlockSpec((k_in, d), lambda i: (0, 0)),
            mod_spec(g1_j),
            pl.BlockSpec((1, d), lambda i: (0, 0)),
            mod_spec(sh_j),
            mod_spec(sc_j),
            pl.BlockSpec((d, ROUTER_PAD), lambda i: (0, 0)),
            pl.BlockSpec((d, ROUTER_PAD), lambda i: (0, 0)),
            pl.BlockSpec((1, ROUTER_PAD), lambda i: (0, 0)),
        ],
        out_specs=[row_spec, pl.BlockSpec((tm * (d // LANES), LANES), lambda i: (i, 0)),
                   pl.BlockSpec((tm, ROUTER_PAD), lambda i: (i, 0))],
        out_shape=[jax.ShapeDtypeStruct((n, d), F32), jax.ShapeDtypeStruct((n * (d // LANES), LANES), F32),
                   jax.ShapeDtypeStruct((n, ROUTER_PAD), F32)],
        scratch_shapes=[pltpu.VMEM((tm, k_in), BF16)],
        compiler_params=_cparams(("arbitrary",)),
        name="oproj_norm_router",
    )(*os, *ls, x, w_o, mod3, ng, mod3, mod3, wr_hi, wr_lo, br)


DMA_UNROLL = 8


def _moe_kernel(te_ref, nu_ref, gsrc_ref, gnxt_ref, sdst_ref, h_hbm, wgu_ref, bgu_ref, wdn_ref,
                bdn_ref, y_hbm, xbuf, ybuf, acc_ref, wgu_b, wdn_b, sem_in, sem_out, *, d_ff, fchunk,
                n_tiles):
    t = pl.program_id(0)
    slot = t % 2
    tm, d = acc_ref.shape
    nsub = d // LANES
    n_used = nu_ref[0]

    def row(ref, r):
        return ref.at[pl.ds(pl.multiple_of(r * nsub, nsub), nsub)]

    def gather(src_ref, sl):
        def body(i, carry):
            for j in range(DMA_UNROLL):
                r = i * DMA_UNROLL + j
                pltpu.make_async_copy(row(h_hbm, src_ref[0, 0, r]), row(xbuf.at[sl], r),
                                      sem_in.at[sl]).start()
            return carry
        lax.fori_loop(0, tm // DMA_UNROLL, body, 0)

    def gather_wait(sl):
        pltpu.make_async_copy(h_hbm.at[pl.ds(0, tm * nsub)], xbuf.at[sl], sem_in.at[sl]).wait()

    def scatter_wait(sl):
        pltpu.make_async_copy(ybuf.at[sl], y_hbm.at[pl.ds(0, tm * nsub)], sem_out.at[sl]).wait()

    @pl.when(t == 0)
    def _():
        ybuf[...] = jnp.zeros_like(ybuf)
        spare = y_hbm.shape[0] - 2 * tm * nsub
        for sl in range(2):
            cp = pltpu.make_async_copy(ybuf.at[sl], y_hbm.at[pl.ds(spare + sl * tm * nsub, tm * nsub)],
                                       sem_out.at[sl])
            cp.start()
            cp.wait()
        gather(gsrc_ref, 0)

    @pl.when(t + 1 < n_used)
    def _():
        gather(gnxt_ref, 1 - slot)

    @pl.when(t < n_used)
    def _():
        @pl.when((t == 0) | (te_ref[t] != te_ref[jnp.maximum(t - 1, 0)]))
        def _():
            rows = 128
            def cast(i, carry):
                r0 = pl.multiple_of(i * rows, rows)
                wgu_b[pl.ds(r0, rows), :] = wgu_ref[0, pl.ds(r0, rows), :].astype(BF16)
                wdn_b[pl.ds(r0, rows), :] = wdn_ref[0, pl.ds(r0, rows), :].astype(BF16)
                return carry
            lax.fori_loop(0, d // rows, cast, 0)

        gather_wait(slot)

        @pl.when(t >= 2)
        def _():
            scatter_wait(slot)

        x = jnp.concatenate([xbuf[slot, pl.ds(j, tm, stride=nsub), :] for j in range(nsub)],
                            axis=1).astype(BF16)
        for c in range(d_ff // fchunk):
            lo, hi = c * fchunk, (c + 1) * fchunk
            gate = jnp.dot(x, wgu_b[:, lo:hi], preferred_element_type=F32) + bgu_ref[0, :, lo:hi]
            up = jnp.dot(x, wgu_b[:, d_ff + lo:d_ff + hi],
                         preferred_element_type=F32) + bgu_ref[0, :, d_ff + lo:d_ff + hi]
            gate = jnp.minimum(gate, SWIGLU_LIMIT)
            up = jnp.clip(up, -SWIGLU_LIMIT, SWIGLU_LIMIT)
            a = ((up + 1.0) * gate * jax.nn.sigmoid(SWIGLU_ALPHA * gate)).astype(BF16)
            part = jnp.dot(a, wdn_b[lo:hi, :], preferred_element_type=F32)
            if c == 0:
                acc_ref[...] = part + bdn_ref[0]
            else:
                acc_ref[...] += part
        for j in range(nsub):
            ybuf[slot, pl.ds(j, tm, stride=nsub), :] = acc_ref[:, j * LANES:(j + 1) * LANES]

        def sbody(i, carry):
            for j in range(DMA_UNROLL):
                r = i * DMA_UNROLL + j
                pltpu.make_async_copy(row(ybuf.at[slot], r), row(y_hbm, sdst_ref[0, 0, r]),
                                      sem_out.at[slot]).start()
            return carry
        lax.fori_loop(0, tm // DMA_UNROLL, sbody, 0)

    @pl.when(t == n_tiles - 1)
    def _():
        for back in (2, 1):
            j = n_used - back

            @pl.when(j >= 0)
            def _(j=j):
                scatter_wait(jnp.maximum(j, 0) % 2)


def _moe_call(h, route, w_gu, b_gu, w_dn, b_dn):
    ne, d, d_ff2 = w_gu.shape
    nsub = d // LANES
    n = h.shape[0] // nsub
    d_ff = d_ff2 // 2
    tm = MOE_TILE
    n_tiles = route["tile_expert"].shape[0]
    kern = functools.partial(_moe_kernel, d_ff=d_ff, fchunk=256, n_tiles=n_tiles)
    idx_spec = lambda shift: pl.BlockSpec(
        (1, 1, tm), lambda t, te, nu: (jnp.minimum(t + shift, n_tiles - 1), 0, 0),
        memory_space=pltpu.SMEM)
    grid_spec = pltpu.PrefetchScalarGridSpec(
        num_scalar_prefetch=2,
        grid=(n_tiles,),
        in_specs=[
            idx_spec(0), idx_spec(1), idx_spec(0),
            pl.BlockSpec(memory_space=pl.ANY),
            pl.BlockSpec((1, d, d_ff2), lambda t, te, nu: (te[t], 0, 0)),
            pl.BlockSpec((1, 1, d_ff2), lambda t, te, nu: (te[t], 0, 0)),
            pl.BlockSpec((1, d_ff, d), lambda t, te, nu: (te[t], 0, 0)),
            pl.BlockSpec((1, 1, d), lambda t, te, nu: (te[t], 0, 0)),
        ],
        out_specs=pl.BlockSpec(memory_space=pl.ANY),
        scratch_shapes=[pltpu.VMEM((2, tm * nsub, LANES), F32), pltpu.VMEM((2, tm * nsub, LANES), F32),
                        pltpu.VMEM((tm, d), F32),
                        pltpu.VMEM((d, d_ff2), BF16), pltpu.VMEM((d_ff, d), BF16),
                        pltpu.SemaphoreType.DMA((2,)), pltpu.SemaphoreType.DMA((2,))],
    )
    return pl.pallas_call(
        kern,
        grid_spec=grid_spec,
        out_shape=jax.ShapeDtypeStruct(((TOP_K * n + 2 * tm) * nsub, LANES), F32),
        compiler_params=_cparams(("arbitrary",)),
        name="moe_experts",
    )(route["tile_expert"], route["n_used"], route["gsrc"], route["gsrc"],
      route["sdst"], h, w_gu, b_gu.reshape(ne, 1, d_ff2), w_dn, b_dn.reshape(ne, 1, d))


def _route(logits, n):
    ne = N_EXPERTS
    tm = MOE_TILE
    top_val, top_idx = lax.top_k(logits[:, :ne], TOP_K)
    gates = jax.nn.softmax(top_val, axis=-1)
    e_flat = top_idx.reshape(-1).astype(jnp.int32)
    nk = n * TOP_K
    counts = jnp.sum((e_flat[:, None] == jnp.arange(ne, dtype=jnp.int32)[None, :]).astype(jnp.int32),
                     axis=0)
    starts = jnp.cumsum(counts) - counts
    padded = (counts + tm - 1) // tm * tm
    pad_ends = jnp.cumsum(padded)
    pad_starts = pad_ends - padded
    n_tiles = nk // tm + ne
    cap = n_tiles * tm
    tile_start = jnp.arange(n_tiles, dtype=jnp.int32) * tm
    tile_expert = jnp.minimum(jnp.searchsorted(pad_ends, tile_start, side="right"),
                              ne - 1).astype(jnp.int32)
    n_used = (pad_ends[-1] // tm).astype(jnp.int32).reshape(1)
    idx_bits = int(np.ceil(np.log2(nk)))
    keys = (e_flat << idx_bits) + jnp.arange(nk, dtype=jnp.int32)
    f_sorted = jnp.sort(keys) & ((1 << idx_bits) - 1)
    slot = jnp.arange(cap, dtype=jnp.int32)
    e_s = tile_expert[slot // tm]
    off = slot - pad_starts[e_s]
    valid = off < counts[e_s]
    f = f_sorted[jnp.clip(starts[e_s] + off, 0, nk - 1)]
    gsrc = jnp.where(valid, f // TOP_K, 0).astype(jnp.int32)
    spare = nk + ((slot // tm) % 2) * tm + slot % tm
    sdst = jnp.where(valid, (f % TOP_K) * n + f // TOP_K, spare).astype(jnp.int32)
    return dict(gates=gates, tile_expert=tile_expert, n_used=n_used,
                gsrc=gsrc.reshape(n_tiles, 1, tm), sdst=sdst.reshape(n_tiles, 1, tm))


def _combine_kernel(x_ref, y0_ref, y1_ref, y2_ref, y3_ref, gt_ref, g2_ref, o_ref):
    tm, d = x_ref.shape
    nsub = d // LANES
    g = gt_ref[...]
    for j in range(nsub):
        y = None
        for k, y_ref in enumerate((y0_ref, y1_ref, y2_ref, y3_ref)):
            term = g[:, k:k + 1] * y_ref[pl.ds(j, tm, stride=nsub), :]
            y = term if y is None else y + term
        cols = slice(j * LANES, (j + 1) * LANES)
        o_ref[:, cols] = x_ref[:, cols] + g2_ref[0, :, cols] * y


def _combine_call(tk, x, ycomb, gates, mod3, g2_j):
    n, d = x.shape
    nsub = d // LANES
    tm = COMBINE_TILE
    nt = n // tm
    brow = lambda i: tk.batch_row(i * tm)
    row_spec = pl.BlockSpec((tm, d), lambda i: (i, 0))
    y_spec = lambda k: pl.BlockSpec((tm * nsub, LANES), lambda i: (k * nt + i, 0))
    return pl.pallas_call(
        _combine_kernel,
        grid=(nt,),
        in_specs=[row_spec, y_spec(0), y_spec(1), y_spec(2), y_spec(3),
                  pl.BlockSpec((tm, TOP_K), lambda i: (i, 0)),
                  pl.BlockSpec((1, 1, d), lambda i: (brow(i), 0, g2_j))],
        out_specs=row_spec,
        out_shape=jax.ShapeDtypeStruct((n, d), F32),
        compiler_params=_cparams(("arbitrary",)),
        name="moe_combine",
    )(x, ycomb, ycomb, ycomb, ycomb, gates, mod3)


def _moe_experts(tk, h, logits, w_gu, b_gu, w_dn, b_dn):
    route = _route(logits, logits.shape[0])
    return _moe_call(h, route, w_gu, b_gu, w_dn, b_dn), route["gates"]


@jax.jit
def _forward(x_prompt, x_sample, c_prompt, c_sample, norm_g, w_mod, b_mod, a_w_qkv, a_q_gain,
             a_k_gain, a_w_o, b_w_qkv, b_q_gain, b_k_gain, b_rpb, b_w_o, w_router, b_router,
             w_gu, b_gu, w_dn, b_dn):
    b_p, s_p, d = x_prompt.shape
    b_s, s_s, _ = x_sample.shape
    tk = _Tokens(b_p, s_p, b_s, s_s)
    depth = w_mod.shape[0]
    x = jnp.concatenate([x_prompt.reshape(-1, d), x_sample.reshape(-1, d)], axis=0)

    nb = b_p + b_s
    rows = -(-nb // 8) * 8
    c_all = jnp.zeros((rows, d), F32).at[:nb].set(jnp.concatenate([c_prompt, c_sample], axis=0))
    mods = _modulation(c_all, w_mod, b_mod)

    cos, sin = _rope_tables(tk.max_s)
    kinds_a = tuple("qkv"[cb // (len(DILATIONS) * HEADS_PER_GROUP_A // 2)]
                    for cb in range(3 * len(DILATIONS) * HEADS_PER_GROUP_A // 2))
    kinds_b = tuple("qkv"[cb // (N_HEADS_B // 2)] for cb in range(3 * N_HEADS_B // 2))

    for layer in range(depth):
        mod3 = mods[layer].reshape(rows, 1, 6 * d)
        j = layer // 2
        if layer % 2 == 0:
            qkv = _qkv_call(tk, x, norm_g[layer, 0].reshape(1, d), mod3, 0, 1,
                            a_w_qkv[j].astype(BF16), a_q_gain[j], a_k_gain[j], cos, sin, kinds_a, True)
            groups = _dilated_attention(tk, qkv)
            os = [g[0] for g in groups]
            ls = [g[1] for g in groups]
            w_o = a_w_o[j]
        else:
            qkv = _qkv_call(tk, x, norm_g[layer, 0].reshape(1, d), mod3, 0, 1,
                            b_w_qkv[j].astype(BF16), b_q_gain[j], b_k_gain[j], cos, sin, kinds_b, False)
            os = [_na_call(tk, qkv, _na_bias_table(b_rpb[j]))]
            ls = []
            w_o = b_w_o[j]
        x, hff, logits = _oproj_call(tk, os, ls, x, w_o.astype(BF16), mod3, 2,
                                     norm_g[layer, 1].reshape(1, d), 3, 4,
                                     w_router[layer], b_router[layer])
        ycomb, gates = _moe_experts(tk, hff, logits, w_gu[layer], b_gu[layer], w_dn[layer], b_dn[layer])
        x = _combine_call(tk, x, ycomb, gates, mod3, 5)

    y_prompt = x[:tk.n_p].reshape(b_p, s_p, d)
    y_sample = x[tk.n_p:].reshape(b_s, s_s, d)
    return y_prompt, y_sample


def kernel(x_prompt, x_sample, c_prompt, c_sample, norm_g, w_mod, b_mod, a_w_qkv, a_q_gain, a_k_gain,
           a_w_o, b_w_qkv, b_q_gain, b_k_gain, b_rpb, b_w_o, w_router, b_router, w_gu, b_gu, w_dn, b_dn):
    return _forward(x_prompt, x_sample, c_prompt, c_sample, norm_g, w_mod, b_mod, a_w_qkv, a_q_gain,
                    a_k_gain, a_w_o, b_w_qkv, b_q_gain, b_k_gain, b_rpb, b_w_o, w_router, b_router,
                    w_gu, b_gu, w_dn, b_dn)
```

```python
import functools

import numpy as np
import jax
import jax.numpy as jnp
from jax import lax
from jax.experimental import pallas as pl
from jax.experimental.pallas import tpu as pltpu

F32 = jnp.float32
BF16 = jnp.bfloat16

HEAD_DIM = 64
ROPE_THETA = 10000.0
NORM_EPS = 1e-6
NEG_INF = -1e30
DILATIONS = (1, 4, 16)
BAND_RADIUS = 64
HEADS_PER_GROUP_A = 8
N_HEADS_B = 16
GRID_W = 64
NA_ROWS = 8
NA_COLS = 16
N_EXPERTS = 32
TOP_K = 4
SWIGLU_LIMIT = 7.0
SWIGLU_ALPHA = 1.702

LANES = 128
VMEM_LIMIT_BYTES = 56 * 1024 * 1024

CHUNK = 2048
ROW_TILE = 512
BAND_TQ = 128
NA_TQ = 512
MOE_TILE = 512
COMBINE_TILE = 256
ROUTER_PAD = LANES


def _cparams(sem):
    return pltpu.CompilerParams(dimension_semantics=sem, vmem_limit_bytes=VMEM_LIMIT_BYTES)


class _Tokens:
    def __init__(self, b_p, s_p, b_s, s_s):
        assert s_p % CHUNK == 0 and s_s % CHUNK == 0
        self.b_p, self.s_p, self.b_s, self.s_s = b_p, s_p, b_s, s_s
        self.n_p = b_p * s_p
        self.n = self.n_p + b_s * s_s
        self.n_chunks = self.n // CHUNK
        self.max_s = max(s_p, s_s)

    def batch_row(self, t0):
        return jnp.where(t0 < self.n_p, t0 // self.s_p, self.b_p + (t0 - self.n_p) // self.s_s)

    def pos(self, t0):
        return jnp.where(t0 < self.n_p, t0 % self.s_p, (t0 - self.n_p) % self.s_s)

    def seq_len(self, t0):
        return jnp.where(t0 < self.n_p, self.s_p, self.s_s)


def _mod_kernel(c_ref, w_ref, b_ref, o_ref):
    c = c_ref[...]
    act = c * jax.nn.sigmoid(c)
    o_ref[0] = jnp.dot(act, w_ref[0], precision=lax.Precision.HIGHEST,
                       preferred_element_type=F32) + b_ref[0]


def _modulation(c_all, w_mod, b_mod):
    depth, d, d6 = w_mod.shape
    rows = c_all.shape[0]
    return pl.pallas_call(
        _mod_kernel,
        grid=(depth, d6 // d),
        in_specs=[
            pl.BlockSpec((rows, d), lambda l, j: (0, 0)),
            pl.BlockSpec((1, d, d), lambda l, j: (l, 0, j)),
            pl.BlockSpec((1, 1, d), lambda l, j: (l, 0, j)),
        ],
        out_specs=pl.BlockSpec((1, rows, d), lambda l, j: (l, 0, j)),
        out_shape=jax.ShapeDtypeStruct((depth, rows, d6), F32),
        compiler_params=_cparams(("arbitrary", "arbitrary")),
        name="modulation",
    )(c_all, w_mod, b_mod.reshape(depth, 1, d6))


def _norm_modulate(x, g, shift, scale):
    y = x * lax.rsqrt(jnp.mean(x * x, axis=-1, keepdims=True) + NORM_EPS)
    return (y * g) * (1.0 + scale) + shift


N_SLABS = 4


def _qkv_kernel(x_ref, g_ref, sh_ref, sc_ref, w_ref, qg_ref, kg_ref, cos_ref, sin_ref, e_ref,
                *rest, kinds, rope, dils):
    n_out = 1 if dils is None else len(dils)
    o_refs, h_ref = rest[:n_out], rest[n_out]
    slab = None if dils is None else rest[n_out + 1]
    tm = x_ref.shape[0]
    hpg = HEADS_PER_GROUP_A // 2
    h_ref[...] = _norm_modulate(x_ref[...], g_ref[...], sh_ref[0], sc_ref[0]).astype(BF16)
    lane = lax.broadcasted_iota(jnp.int32, (1, LANES), 1)
    first_half = (lane % HEAD_DIM) < (HEAD_DIM // 2)
    n_strided = 0
    for c in range(len(kinds) // 2):
        acc = jnp.dot(h_ref[...], w_ref[:, c * 2 * LANES:(c + 1) * 2 * LANES],
                      preferred_element_type=F32)
        if kinds[2 * c] != "v":
            ss = jnp.dot((acc * acc).astype(BF16), e_ref[...], preferred_element_type=F32)
            acc = acc * lax.rsqrt(ss * (1.0 / HEAD_DIM) + NORM_EPS)
        for half in range(2):
            cb = 2 * c + half
            y = acc[:, half * LANES:(half + 1) * LANES]
            if kinds[cb] != "v":
                y = y * (qg_ref[...] if kinds[cb] == "q" else kg_ref[...])
                if rope:
                    partner = jnp.where(first_half, pltpu.roll(y, LANES - HEAD_DIM // 2, 1),
                                        pltpu.roll(y, HEAD_DIM // 2, 1))
                    y = y * cos_ref[...] + partner * sin_ref[...]
            if dils is None:
                o_refs[0][cb] = y.astype(BF16)
                continue
            ng = len(dils)
            kind, g, hp = cb // (ng * hpg), (cb % (ng * hpg)) // hpg, cb % hpg
            idx, dil = kind * hpg + hp, dils[g]
            if dil == 1:
                o_refs[g][idx, 0, 0] = y.astype(BF16)
            else:
                s = n_strided % N_SLABS
                n_strided += 1
                slab[s] = y
                for p in range(dil):
                    o_refs[g][idx, 0, p] = slab[s, pl.ds(p, tm // dil, stride=dil), :].astype(BF16)


def _qkv_call(tk, x, g, mod3, shift_j, scale_j, w, q_gain, k_gain, cos, sin, kinds, rope, dils=None):
    n, d = x.shape
    ncols = w.shape[1]
    ncb = ncols // LANES
    tm = ROW_TILE
    if dils is None:
        out_specs = pl.BlockSpec((ncb, tm, LANES), lambda i: (0, i, 0))
        out_shape = jax.ShapeDtypeStruct((ncb, n, LANES), BF16)
        scratch = [pltpu.VMEM((tm, d), BF16)]
    else:
        tpc = CHUNK // tm
        gcb = ncb // len(dils)
        out_specs = [pl.BlockSpec((gcb, 1, dil, tm // dil, LANES),
                                  lambda i: (0, i // tpc, 0, i % tpc, 0)) for dil in dils]
        out_shape = [jax.ShapeDtypeStruct((gcb, n // CHUNK, dil, CHUNK // dil, LANES), BF16)
                     for dil in dils]
        scratch = [pltpu.VMEM((tm, d), BF16), pltpu.VMEM((N_SLABS, tm, LANES), F32)]
    assert all(kinds[2 * c] == kinds[2 * c + 1] for c in range(ncb // 2))
    ones = np.kron(np.eye(2 * LANES // HEAD_DIM), np.ones((HEAD_DIM, HEAD_DIM))).astype(np.float32)
    e = jnp.asarray(ones, BF16)
    scale = HEAD_DIM ** -0.5
    qg = jnp.tile(q_gain.reshape(1, HEAD_DIM), (1, LANES // HEAD_DIM)) * scale
    kg = jnp.tile(k_gain.reshape(1, HEAD_DIM), (1, LANES // HEAD_DIM))
    brow = lambda i: tk.batch_row(i * tm)
    prow = lambda i: tk.pos(i * tm) // tm
    kern = functools.partial(_qkv_kernel, kinds=kinds, rope=rope, dils=dils)
    return pl.pallas_call(
        kern,
        grid=(n // tm,),
        in_specs=[
            pl.BlockSpec((tm, d), lambda i: (i, 0)),
            pl.BlockSpec((1, d), lambda i: (0, 0)),
            pl.BlockSpec((1, 1, d), lambda i: (brow(i), 0, shift_j)),
            pl.BlockSpec((1, 1, d), lambda i: (brow(i), 0, scale_j)),
            pl.BlockSpec((d, ncols), lambda i: (0, 0)),
            pl.BlockSpec((1, LANES), lambda i: (0, 0)),
            pl.BlockSpec((1, LANES), lambda i: (0, 0)),
            pl.BlockSpec((tm, LANES), lambda i: (prow(i), 0)),
            pl.BlockSpec((tm, LANES), lambda i: (prow(i), 0)),
            pl.BlockSpec((2 * LANES, 2 * LANES), lambda i: (0, 0)),
        ],
        out_specs=out_specs,
        out_shape=out_shape,
        scratch_shapes=scratch,
        compiler_params=_cparams(("arbitrary",)),
        name="norm_qkv",
    )(x, g, mod3, mod3, w, qg, kg, cos, sin, e)


def _rope_tables(max_s):
    half = HEAD_DIM // 2
    inv = ROPE_THETA ** (-np.arange(half, dtype=np.float64) / half)
    ang = np.arange(max_s, dtype=np.float64)[:, None] * inv[None, :]
    cos = np.cos(ang)
    sin = np.sin(ang)
    cos_head = np.concatenate([cos, cos], axis=1)
    sin_head = np.concatenate([-sin, sin], axis=1)
    reps = LANES // HEAD_DIM
    return (jnp.asarray(np.tile(cos_head, (1, reps)), F32),
            jnp.asarray(np.tile(sin_head, (1, reps)), F32))


def _band_kernel(q_ref, kp_ref, kc_ref, kn_ref, vp_ref, vc_ref, vn_ref, o_ref, lse_ref, *,
                 lc, chunks_p, chunks_per_seq_p):
    c = pl.program_id(0)
    i = pl.program_id(2)
    tq = BAND_TQ
    r = BAND_RADIUS
    kw = tq + 2 * r
    in_prompt = c < chunks_p
    cofs = jnp.where(in_prompt, c % chunks_per_seq_p, 0)
    seq_rows = jnp.where(in_prompt, chunks_per_seq_p, 1) * lc
    u0 = cofs * lc + i * tq
    row = lax.broadcasted_iota(jnp.int32, (tq, kw), 0)
    col = lax.broadcasted_iota(jnp.int32, (tq, kw), 1)
    uk = u0 - r + col
    valid = (jnp.abs(col - r - row) <= r) & (uk >= 0) & (uk < seq_rows)
    lane = lax.broadcasted_iota(jnp.int32, (1, LANES), 1)
    lo = lane < HEAD_DIM
    hi = jnp.logical_not(lo)
    nhp = q_ref.shape[0]
    qs, ks, vs = [], [], []
    for hp in range(nhp):
        q = q_ref[hp, 0, 0]
        k = jnp.concatenate([kp_ref[hp, 0, 0], kc_ref[hp, 0, 0], kn_ref[hp, 0, 0]], axis=0)
        v = jnp.concatenate([vp_ref[hp, 0, 0], vc_ref[hp, 0, 0], vn_ref[hp, 0, 0]], axis=0)
        qs += [jnp.where(lo, q, jnp.zeros_like(q)), jnp.where(hi, q, jnp.zeros_like(q))]
        ks += [k, k]
        vs += [v, v]
    s = lax.dot_general(jnp.stack(qs), jnp.stack(ks), (((2,), (2,)), ((0,), (0,))),
                        preferred_element_type=F32)
    s = jnp.where(valid[None], s, NEG_INF)
    m = jnp.max(s, axis=-1, keepdims=True)
    p = jnp.exp(s - m)
    den = jnp.sum(p, axis=-1, keepdims=True)
    o = lax.dot_general(p.astype(BF16), jnp.stack(vs), (((2,), (1,)), ((0,), (0,))),
                        preferred_element_type=F32) / den
    lse = m + jnp.log(den)
    for hp in range(nhp):
        o_ref[hp, 0, 0] = jnp.where(lo, o[2 * hp], o[2 * hp + 1]).astype(BF16)
        lse_ref[hp, 0, 0] = jnp.where(lo, lse[2 * hp], lse[2 * hp + 1])


def _band_call(tk, arr, q_blk, k_blk, v_blk, dil):
    nch = tk.n_chunks
    lc = CHUNK // dil
    tq = BAND_TQ
    r = BAND_RADIUS
    hpg = HEADS_PER_GROUP_A // 2
    per = tq // r
    nb = lc // r

    def cur(blk):
        return pl.BlockSpec((hpg, 1, 1, tq, LANES), lambda c, p, i: (blk, c, p, i, 0))

    def prev(blk):
        def imap(c, p, i):
            j = per * i - 1
            return (blk, jnp.where(j < 0, jnp.maximum(c - 1, 0), c), p, jnp.where(j < 0, nb - 1, j), 0)
        return pl.BlockSpec((hpg, 1, 1, r, LANES), imap)

    def nxt(blk):
        def imap(c, p, i):
            j = per * (i + 1)
            return (blk, jnp.where(j >= nb, jnp.minimum(c + 1, nch - 1), c), p,
                    jnp.where(j >= nb, 0, j), 0)
        return pl.BlockSpec((hpg, 1, 1, r, LANES), imap)

    kern = functools.partial(_band_kernel, lc=lc, chunks_p=tk.n_p // CHUNK,
                             chunks_per_seq_p=tk.s_p // CHUNK)
    out_spec = pl.BlockSpec((hpg, 1, 1, tq, LANES), lambda c, p, i: (0, c, p, i, 0))
    return pl.pallas_call(
        kern,
        grid=(nch, dil, lc // tq),
        in_specs=[cur(q_blk), prev(k_blk), cur(k_blk), nxt(k_blk), prev(v_blk), cur(v_blk), nxt(v_blk)],
        out_specs=[out_spec, out_spec],
        out_shape=[jax.ShapeDtypeStruct((hpg, nch, dil, lc, LANES), BF16),
                   jax.ShapeDtypeStruct((hpg, nch, dil, lc, LANES), F32)],
        compiler_params=_cparams(("arbitrary", "arbitrary", "arbitrary")),
        name="band_attention",
    )(arr, arr, arr, arr, arr, arr, arr)


def _dilated_attention(tk, qkv_groups):
    return [_band_call(tk, arr, 0, 1, 2, dil) for arr, dil in zip(qkv_groups, DILATIONS)]


def _na_kernel(q_ref, kp_ref, kc_ref, kn_ref, vp_ref, vc_ref, vn_ref, bias_ref, o_ref, kbuf, vbuf, *,
               rows_p, rows_s, blocks_p):
    bq = pl.program_id(1)
    tq = NA_TQ
    rows_per_blk = tq // GRID_W
    in_prompt = bq < blocks_p
    rows = jnp.where(in_prompt, rows_p, rows_s)
    blk_in_seq = jnp.where(in_prompt, bq % (rows_p // rows_per_blk),
                           (bq - blocks_p) % (rows_s // rows_per_blk))
    r0 = blk_in_seq * rows_per_blk
    kbuf[0 * tq:1 * tq] = kp_ref[0]
    kbuf[1 * tq:2 * tq] = kc_ref[0]
    kbuf[2 * tq:3 * tq] = kn_ref[0]
    vbuf[0 * tq:1 * tq] = vp_ref[0]
    vbuf[1 * tq:2 * tq] = vc_ref[0]
    vbuf[2 * tq:3 * tq] = vn_ref[0]
    lane = lax.broadcasted_iota(jnp.int32, (1, LANES), 1)
    lo = lane < HEAD_DIM
    hi = jnp.logical_not(lo)
    nkeys = NA_ROWS * GRID_W
    qs, ks, vs, bs = [], [], [], []
    for d in range(rows_per_blk):
        r = r0 + d
        rs = jnp.clip(r - NA_ROWS // 2, 0, rows - NA_ROWS)
        delta = r - rs
        off = pl.multiple_of((rs - (r0 - rows_per_blk)) * GRID_W, GRID_W)
        k = kbuf[pl.ds(off, nkeys), :]
        v = vbuf[pl.ds(off, nkeys), :]
        q = q_ref[0, d * GRID_W:(d + 1) * GRID_W, :]
        qs += [jnp.where(lo, q, jnp.zeros_like(q)), jnp.where(hi, q, jnp.zeros_like(q))]
        ks += [k, k]
        vs += [v, v]
        bs += [bias_ref[0, 0, delta], bias_ref[0, 1, delta]]
    s = lax.dot_general(jnp.stack(qs), jnp.stack(ks), (((2,), (2,)), ((0,), (0,))),
                        preferred_element_type=F32)
    s = s + jnp.stack(bs)
    m = jnp.max(s, axis=-1, keepdims=True)
    p = jnp.exp(s - m)
    den = jnp.sum(p, axis=-1, keepdims=True)
    o = lax.dot_general(p.astype(BF16), jnp.stack(vs), (((2,), (1,)), ((0,), (0,))),
                        preferred_element_type=F32) / den
    for d in range(rows_per_blk):
        o_ref[0, d * GRID_W:(d + 1) * GRID_W, :] = jnp.where(lo, o[2 * d], o[2 * d + 1]).astype(BF16)


def _na_bias_table(rpb):
    qc = np.arange(GRID_W)
    cstart = np.clip(qc - NA_COLS // 2, 0, GRID_W - NA_COLS)
    kc = np.arange(GRID_W)
    col_valid = (kc[None, :] >= cstart[:, None]) & (kc[None, :] < cstart[:, None] + NA_COLS)
    col_off = np.clip(kc[None, :] - qc[:, None] + NA_COLS - 1, 0, 2 * NA_COLS - 2)
    delta = np.arange(NA_ROWS)
    row_off = np.arange(NA_ROWS)[None, :] - delta[:, None] + NA_ROWS - 1
    t = rpb[:, row_off]
    t = t[:, :, :, col_off]
    t = jnp.where(jnp.asarray(col_valid)[None, None, None], t, NEG_INF)
    t = jnp.transpose(t, (0, 1, 3, 2, 4))
    h = rpb.shape[0]
    return t.reshape(h // 2, 2, NA_ROWS, GRID_W, NA_ROWS * GRID_W).astype(F32)


def _na_call(tk, qkv, bias):
    n = tk.n
    tq = NA_TQ
    nhp = N_HEADS_B // 2
    nblk = n // tq
    arr = qkv.reshape(3 * nhp, nblk, tq, LANES)
    nkeys = NA_ROWS * GRID_W

    arr3 = arr
    kern = functools.partial(_na_kernel, rows_p=tk.s_p // GRID_W, rows_s=tk.s_s // GRID_W,
                             blocks_p=tk.n_p // tq)

    def bspec(base, shift):
        return pl.BlockSpec((None, 1, tq, LANES),
                            lambda hp, b: (base + hp, jnp.clip(b + shift, 0, nblk - 1), 0, 0))

    return pl.pallas_call(
        kern,
        grid=(nhp, nblk),
        in_specs=[bspec(0, 0), bspec(nhp, -1), bspec(nhp, 0), bspec(nhp, 1),
                  bspec(2 * nhp, -1), bspec(2 * nhp, 0), bspec(2 * nhp, 1),
                  pl.BlockSpec((1, 2, NA_ROWS, GRID_W, nkeys), lambda hp, b: (hp, 0, 0, 0, 0))],
        out_specs=pl.BlockSpec((None, 1, tq, LANES), lambda hp, b: (hp, b, 0, 0)),
        out_shape=jax.ShapeDtypeStruct((nhp, nblk, tq, LANES), BF16),
        scratch_shapes=[pltpu.VMEM((3 * tq, LANES), BF16), pltpu.VMEM((3 * tq, LANES), BF16)],
        compiler_params=_cparams(("arbitrary", "arbitrary")),
        name="neighbourhood_attention",
    )(arr3, arr3, arr3, arr3, arr3, arr3, arr3, bias).reshape(nhp, n, LANES)


def _oproj_kernel(*refs, dils, n_tiles):
    n_groups = 1 if dils is None else len(dils)
    o_refs = refs[:n_groups]
    l_refs = refs[n_groups:2 * n_groups] if dils is not None else ()
    k = len(o_refs) + len(l_refs)
    (x_ref, wo_ref, g1_ref, ng_ref, sh_ref, sc_ref, wrh_ref, wrl_ref, br_ref,
     xo_ref, h_ref, lg_ref, m_ref) = refs[k:k + 13]
    slab = refs[k + 13] if dils is not None else None
    tm = x_ref.shape[0]
    n_strided = [0]

    def natural(ref, t, dil):
        if dil == 1:
            return ref[t, 0, 0].astype(F32)
        s = n_strided[0] % N_SLABS
        n_strided[0] += 1
        for p in range(dil):
            slab[s, pl.ds(p, tm // dil, stride=dil), :] = ref[t, 0, p].astype(F32)
        return slab[s]

    for t in range(n_tiles):
        if dils is None:
            m_ref[:, t * LANES:(t + 1) * LANES] = o_refs[0][t]
        else:
            ls = [natural(l, t, dil) for l, dil in zip(l_refs, dils)]
            mx = functools.reduce(jnp.maximum, ls)
            es = [jnp.exp(l - mx) for l in ls]
            den = functools.reduce(lambda a, b: a + b, es)
            num = functools.reduce(lambda a, b: a + b,
                                   [e * natural(o, t, dil) for e, o, dil in zip(es, o_refs, dils)])
            m_ref[:, t * LANES:(t + 1) * LANES] = (num / den).astype(BF16)
    mix = jnp.dot(m_ref[...], wo_ref[...], preferred_element_type=F32)
    x = x_ref[...] + g1_ref[0] * mix
    xo_ref[...] = x
    h = _norm_modulate(x, ng_ref[...], sh_ref[0], sc_ref[0])
    hi = h.astype(BF16)
    lo = (h - hi.astype(F32)).astype(BF16)
    tm, d = h.shape
    nsub = d // LANES
    for j in range(nsub):
        h_ref[pl.ds(j, tm, stride=nsub), :] = h[:, j * LANES:(j + 1) * LANES]
    lg = jnp.dot(hi, wrh_ref[...], preferred_element_type=F32)
    lg = lg + jnp.dot(lo, wrh_ref[...], preferred_element_type=F32)
    lg = lg + jnp.dot(hi, wrl_ref[...], preferred_element_type=F32)
    lg_ref[...] = lg + br_ref[...]


def _oproj_call(tk, os, ls, x, w_o, mod3, g1_j, ng, sh_j, sc_j, w_router, b_router, dils=None):
    n, d = x.shape
    tm = ROW_TILE
    n_tiles = os[0].shape[0]
    k_in = n_tiles * LANES
    if dils is None:
        tile_specs = [pl.BlockSpec((n_tiles, tm, LANES), lambda i: (0, i, 0))]
        scratch = [pltpu.VMEM((tm, k_in), BF16)]
    else:
        tpc = CHUNK // tm
        tile_specs = [pl.BlockSpec((n_tiles, 1, dil, tm // dil, LANES),
                                   lambda i: (0, i // tpc, 0, i % tpc, 0)) for dil in dils] * 2
        scratch = [pltpu.VMEM((tm, k_in), BF16), pltpu.VMEM((N_SLABS, tm, LANES), F32)]
    ne = w_router.shape[1]
    wr = jnp.zeros((d, ROUTER_PAD), F32).at[:, :ne].set(w_router)
    wr_hi = wr.astype(BF16)
    wr_lo = (wr - wr_hi.astype(F32)).astype(BF16)
    br = jnp.zeros((1, ROUTER_PAD), F32).at[0, :ne].set(b_router)
    brow = lambda i: tk.batch_row(i * tm)
    row_spec = pl.BlockSpec((tm, d), lambda i: (i, 0))
    mod_spec = lambda j: pl.BlockSpec((1, 1, d), lambda i: (brow(i), 0, j))
    kern = functools.partial(_oproj_kernel, dils=dils, n_tiles=n_tiles)
    return pl.pallas_call(
        kern,
        grid=(n // tm,),
        in_specs=tile_specs + [
            row_spec,
            pl.BlockSpec((k_in, d), lambda i: (0, 0)),
            mod_spec(g1_j),
            pl.BlockSpec((1, d), lambda i: (0, 0)),
            mod_spec(sh_j),
            mod_spec(sc_j),
            pl.BlockSpec((d, ROUTER_PAD), lambda i: (0, 0)),
            pl.BlockSpec((d, ROUTER_PAD), lambda i: (0, 0)),
            pl.BlockSpec((1, ROUTER_PAD), lambda i: (0, 0)),
        ],
        out_specs=[row_spec, pl.BlockSpec((tm * (d // LANES), LANES), lambda i: (i, 0)),
                   pl.BlockSpec((tm, ROUTER_PAD), lambda i: (i, 0))],
        out_shape=[jax.ShapeDtypeStruct((n, d), F32), jax.ShapeDtypeStruct((n * (d // LANES), LANES), F32),
                   jax.ShapeDtypeStruct((n, ROUTER_PAD), F32)],
        scratch_shapes=scratch,
        compiler_params=_cparams(("arbitrary",)),
        name="oproj_norm_router",
    )(*os, *ls, x, w_o, mod3, ng, mod3, mod3, wr_hi, wr_lo, br)


DMA_UNROLL = 16


def _moe_kernel(te_ref, nu_ref, gsrc_ref, gnxt_ref, sdst_ref, h_hbm, wgu_ref, bgu_ref, wdn_ref,
                bdn_ref, y_hbm, xbuf, ybuf, acc_ref, wgu_b, wdn_b, sem_in, sem_out, *, d_ff, fchunk,
                n_tiles):
    t = pl.program_id(0)
    slot = t % 2
    tm, d = acc_ref.shape
    nsub = d // LANES
    n_used = nu_ref[0]

    def row(ref, r):
        return ref.at[pl.ds(pl.multiple_of(r * nsub, nsub), nsub)]

    def gather(src_ref, sl):
        def body(i, carry):
            for j in range(DMA_UNROLL):
                r = i * DMA_UNROLL + j
                pltpu.make_async_copy(row(h_hbm, src_ref[0, 0, r]), row(xbuf.at[sl], r),
                                      sem_in.at[sl]).start()
            return carry
        lax.fori_loop(0, tm // DMA_UNROLL, body, 0)

    def gather_wait(sl):
        pltpu.make_async_copy(h_hbm.at[pl.ds(0, tm * nsub)], xbuf.at[sl], sem_in.at[sl]).wait()

    def scatter_wait(sl):
        pltpu.make_async_copy(ybuf.at[sl], y_hbm.at[pl.ds(0, tm * nsub)], sem_out.at[sl]).wait()

    @pl.when(t == 0)
    def _():
        ybuf[...] = jnp.zeros_like(ybuf)
        spare = y_hbm.shape[0] - 2 * tm * nsub
        for sl in range(2):
            cp = pltpu.make_async_copy(ybuf.at[sl], y_hbm.at[pl.ds(spare + sl * tm * nsub, tm * nsub)],
                                       sem_out.at[sl])
            cp.start()
            cp.wait()
        gather(gsrc_ref, 0)

    @pl.when(t + 1 < n_used)
    def _():
        gather(gnxt_ref, 1 - slot)

    @pl.when(t < n_used)
    def _():
        @pl.when((t == 0) | (te_ref[t] != te_ref[jnp.maximum(t - 1, 0)]))
        def _():
            rows = 128
            def cast(i, carry):
                r0 = pl.multiple_of(i * rows, rows)
                wgu_b[pl.ds(r0, rows), :] = wgu_ref[0, pl.ds(r0, rows), :].astype(BF16)
                wdn_b[pl.ds(r0, rows), :] = wdn_ref[0, pl.ds(r0, rows), :].astype(BF16)
                return carry
            lax.fori_loop(0, d // rows, cast, 0)

        gather_wait(slot)

        @pl.when(t >= 2)
        def _():
            scatter_wait(slot)

        x = jnp.concatenate([xbuf[slot, pl.ds(j, tm, stride=nsub), :] for j in range(nsub)],
                            axis=1).astype(BF16)
        for c in range(d_ff // fchunk):
            lo, hi = c * fchunk, (c + 1) * fchunk
            gate = jnp.dot(x, wgu_b[:, lo:hi], preferred_element_type=F32) + bgu_ref[0, :, lo:hi]
            up = jnp.dot(x, wgu_b[:, d_ff + lo:d_ff + hi],
                         preferred_element_type=F32) + bgu_ref[0, :, d_ff + lo:d_ff + hi]
            gate = jnp.minimum(gate, SWIGLU_LIMIT)
            up = jnp.clip(up, -SWIGLU_LIMIT, SWIGLU_LIMIT)
            a = ((up + 1.0) * gate * jax.nn.sigmoid(SWIGLU_ALPHA * gate)).astype(BF16)
            part = jnp.dot(a, wdn_b[lo:hi, :], preferred_element_type=F32)
            if c == 0:
                acc_ref[...] = part + bdn_ref[0]
            else:
                acc_ref[...] += part
        for j in range(nsub):
            ybuf[slot, pl.ds(j, tm, stride=nsub), :] = acc_ref[:, j * LANES:(j + 1) * LANES]

        def sbody(i, carry):
            for j in range(DMA_UNROLL):
                r = i * DMA_UNROLL + j
                pltpu.make_async_copy(row(ybuf.at[slot], r), row(y_hbm, sdst_ref[0, 0, r]),
                                      sem_out.at[slot]).start()
            return carry
        lax.fori_loop(0, tm // DMA_UNROLL, sbody, 0)

    @pl.when(t == n_tiles - 1)
    def _():
        for back in (2, 1):
            j = n_used - back

            @pl.when(j >= 0)
            def _(j=j):
                scatter_wait(jnp.maximum(j, 0) % 2)


def _moe_call(h, route, w_gu, b_gu, w_dn, b_dn):
    ne, d, d_ff2 = w_gu.shape
    nsub = d // LANES
    n = h.shape[0] // nsub
    d_ff = d_ff2 // 2
    tm = MOE_TILE
    n_tiles = route["tile_expert"].shape[0]
    kern = functools.partial(_moe_kernel, d_ff=d_ff, fchunk=256, n_tiles=n_tiles)
    idx_spec = lambda shift: pl.BlockSpec(
        (1, 1, tm), lambda t, te, nu: (jnp.minimum(t + shift, n_tiles - 1), 0, 0),
        memory_space=pltpu.SMEM)
    grid_spec = pltpu.PrefetchScalarGridSpec(
        num_scalar_prefetch=2,
        grid=(n_tiles,),
        in_specs=[
            idx_spec(0), idx_spec(1), idx_spec(0),
            pl.BlockSpec(memory_space=pl.ANY),
            pl.BlockSpec((1, d, d_ff2), lambda t, te, nu: (te[t], 0, 0)),
            pl.BlockSpec((1, 1, d_ff2), lambda t, te, nu: (te[t], 0, 0)),
            pl.BlockSpec((1, d_ff, d), lambda t, te, nu: (te[t], 0, 0)),
            pl.BlockSpec((1, 1, d), lambda t, te, nu: (te[t], 0, 0)),
        ],
        out_specs=pl.BlockSpec(memory_space=pl.ANY),
        scratch_shapes=[pltpu.VMEM((2, tm * nsub, LANES), F32), pltpu.VMEM((2, tm * nsub, LANES), F32),
                        pltpu.VMEM((tm, d), F32),
                        pltpu.VMEM((d, d_ff2), BF16), pltpu.VMEM((d_ff, d), BF16),
                        pltpu.SemaphoreType.DMA((2,)), pltpu.SemaphoreType.DMA((2,))],
    )
    return pl.pallas_call(
        kern,
        grid_spec=grid_spec,
        out_shape=jax.ShapeDtypeStruct(((TOP_K * n + 2 * tm) * nsub, LANES), F32),
        compiler_params=_cparams(("arbitrary",)),
        name="moe_experts",
    )(route["tile_expert"], route["n_used"], route["gsrc"], route["gsrc"],
      route["sdst"], h, w_gu, b_gu.reshape(ne, 1, d_ff2), w_dn, b_dn.reshape(ne, 1, d))


def _route(logits, n):
    ne = N_EXPERTS
    tm = MOE_TILE
    top_val, top_idx = lax.top_k(logits[:, :ne], TOP_K)
    gates = jax.nn.softmax(top_val, axis=-1)
    e_flat = top_idx.reshape(-1).astype(jnp.int32)
    nk = n * TOP_K
    counts = jnp.sum((e_flat[:, None] == jnp.arange(ne, dtype=jnp.int32)[None, :]).astype(jnp.int32),
                     axis=0)
    padded = (counts + tm - 1) // tm * tm
    pad_ends = jnp.cumsum(padded)
    n_tiles = nk // tm + ne
    cap = n_tiles * tm
    tile_start = jnp.arange(n_tiles, dtype=jnp.int32) * tm
    tile_expert = jnp.minimum(jnp.searchsorted(pad_ends, tile_start, side="right"),
                              ne - 1).astype(jnp.int32)
    n_used = (pad_ends[-1] // tm).astype(jnp.int32).reshape(1)
    idx_bits = int(np.ceil(np.log2(nk + tm)))
    pad_base = (1 << idx_bits) - tm
    real_keys = (e_flat << idx_bits) + jnp.arange(nk, dtype=jnp.int32)
    lane = jnp.arange(tm, dtype=jnp.int32)[None, :]
    expert = jnp.arange(ne, dtype=jnp.int32)[:, None]
    pad_keys = jnp.where(lane < (padded - counts)[:, None], (expert << idx_bits) + pad_base + lane,
                         jnp.iinfo(jnp.int32).max)
    low = jnp.sort(jnp.concatenate([real_keys, pad_keys.reshape(-1)])) & ((1 << idx_bits) - 1)
    valid = low < pad_base
    slot = jnp.arange(cap, dtype=jnp.int32)
    gsrc = jnp.where(valid, low // TOP_K, 0).astype(jnp.int32)
    spare = nk + ((slot // tm) % 2) * tm + slot % tm
    sdst = jnp.where(valid, (low % TOP_K) * n + low // TOP_K, spare).astype(jnp.int32)
    return dict(gates=gates, tile_expert=tile_expert, n_used=n_used,
                gsrc=gsrc.reshape(n_tiles, 1, tm), sdst=sdst.reshape(n_tiles, 1, tm))


def _combine_kernel(x_ref, y0_ref, y1_ref, y2_ref, y3_ref, gt_ref, g2_ref, o_ref):
    tm, d = x_ref.shape
    nsub = d // LANES
    g = gt_ref[...]
    for j in range(nsub):
        y = None
        for k, y_ref in enumerate((y0_ref, y1_ref, y2_ref, y3_ref)):
            term = g[:, k:k + 1] * y_ref[pl.ds(j, tm, stride=nsub), :]
            y = term if y is None else y + term
        cols = slice(j * LANES, (j + 1) * LANES)
        o_ref[:, cols] = x_ref[:, cols] + g2_ref[0, :, cols] * y


def _combine_call(tk, x, ycomb, gates, mod3, g2_j):
    n, d = x.shape
    nsub = d // LANES
    tm = COMBINE_TILE
    nt = n // tm
    brow = lambda i: tk.batch_row(i * tm)
    row_spec = pl.BlockSpec((tm, d), lambda i: (i, 0))
    y_spec = lambda k: pl.BlockSpec((tm * nsub, LANES), lambda i: (k * nt + i, 0))
    return pl.pallas_call(
        _combine_kernel,
        grid=(nt,),
        in_specs=[row_spec, y_spec(0), y_spec(1), y_spec(2), y_spec(3),
                  pl.BlockSpec((tm, TOP_K), lambda i: (i, 0)),
                  pl.BlockSpec((1, 1, d), lambda i: (brow(i), 0, g2_j))],
        out_specs=row_spec,
        out_shape=jax.ShapeDtypeStruct((n, d), F32),
        compiler_params=_cparams(("arbitrary",)),
        name="moe_combine",
    )(x, ycomb, ycomb, ycomb, ycomb, gates, mod3)


def _moe_experts(tk, h, logits, w_gu, b_gu, w_dn, b_dn):
    route = _route(logits, logits.shape[0])
    return _moe_call(h, route, w_gu, b_gu, w_dn, b_dn), route["gates"]


@jax.jit
def _forward(x_prompt, x_sample, c_prompt, c_sample, norm_g, w_mod, b_mod, a_w_qkv, a_q_gain,
             a_k_gain, a_w_o, b_w_qkv, b_q_gain, b_k_gain, b_rpb, b_w_o, w_router, b_router,
             w_gu, b_gu, w_dn, b_dn):
    b_p, s_p, d = x_prompt.shape
    b_s, s_s, _ = x_sample.shape
    tk = _Tokens(b_p, s_p, b_s, s_s)
    depth = w_mod.shape[0]
    x = jnp.concatenate([x_prompt.reshape(-1, d), x_sample.reshape(-1, d)], axis=0)

    nb = b_p + b_s
    rows = -(-nb // 8) * 8
    c_all = jnp.zeros((rows, d), F32).at[:nb].set(jnp.concatenate([c_prompt, c_sample], axis=0))
    mods = _modulation(c_all, w_mod, b_mod)

    cos, sin = _rope_tables(tk.max_s)
    kinds_a = tuple("qkv"[cb // (len(DILATIONS) * HEADS_PER_GROUP_A // 2)]
                    for cb in range(3 * len(DILATIONS) * HEADS_PER_GROUP_A // 2))
    kinds_b = tuple("qkv"[cb // (N_HEADS_B // 2)] for cb in range(3 * N_HEADS_B // 2))

    for layer in range(depth):
        mod3 = mods[layer].reshape(rows, 1, 6 * d)
        j = layer // 2
        if layer % 2 == 0:
            qkv = _qkv_call(tk, x, norm_g[layer, 0].reshape(1, d), mod3, 0, 1,
                            a_w_qkv[j].astype(BF16), a_q_gain[j], a_k_gain[j], cos, sin, kinds_a, True,
                            dils=DILATIONS)
            groups = _dilated_attention(tk, qkv)
            os = [g[0] for g in groups]
            ls = [g[1] for g in groups]
            w_o = a_w_o[j]
            dils = DILATIONS
        else:
            dils = None
            qkv = _qkv_call(tk, x, norm_g[layer, 0].reshape(1, d), mod3, 0, 1,
                            b_w_qkv[j].astype(BF16), b_q_gain[j], b_k_gain[j], cos, sin, kinds_b, False)
            os = [_na_call(tk, qkv, _na_bias_table(b_rpb[j]))]
            ls = []
            w_o = b_w_o[j]
        x, hff, logits = _oproj_call(tk, os, ls, x, w_o.astype(BF16), mod3, 2,
                                     norm_g[layer, 1].reshape(1, d), 3, 4,
                                     w_router[layer], b_router[layer], dils=dils)
        ycomb, gates = _moe_experts(tk, hff, logits, w_gu[layer], b_gu[layer], w_dn[layer], b_dn[layer])
        x = _combine_call(tk, x, ycomb, gates, mod3, 5)

    y_prompt = x[:tk.n_p].reshape(b_p, s_p, d)
    y_sample = x[tk.n_p:].reshape(b_s, s_s, d)
    return y_prompt, y_sample


def kernel(x_prompt, x_sample, c_prompt, c_sample, norm_g, w_mod, b_mod, a_w_qkv, a_q_gain, a_k_gain,
           a_w_o, b_w_qkv, b_q_gain, b_k_gain, b_rpb, b_w_o, w_router, b_router, w_gu, b_gu, w_dn, b_dn):
    return _forward(x_prompt, x_sample, c_prompt, c_sample, norm_g, w_mod, b_mod, a_w_qkv, a_q_gain,
                    a_k_gain, a_w_o, b_w_qkv, b_q_gain, b_k_gain, b_rpb, b_w_o, w_router, b_router,
                    w_gu, b_gu, w_dn, b_dn)
```

```python
import functools

import numpy as np
import jax
import jax.numpy as jnp
from jax import lax
from jax.experimental import pallas as pl
from jax.experimental.pallas import tpu as pltpu

F32 = jnp.float32
BF16 = jnp.bfloat16

HEAD_DIM = 64
ROPE_THETA = 10000.0
NORM_EPS = 1e-6
NEG_INF = -1e30
DILATIONS = (1, 4, 16)
BAND_RADIUS = 64
HEADS_PER_GROUP_A = 8
N_HEADS_B = 16
GRID_W = 64
NA_ROWS = 8
NA_COLS = 16
N_EXPERTS = 32
TOP_K = 4
SWIGLU_LIMIT = 7.0
SWIGLU_ALPHA = 1.702

LANES = 128
VMEM_LIMIT_BYTES = 56 * 1024 * 1024

CHUNK = 2048
ROW_TILE = 512
BAND_TQ = 128
NA_TQ = 512
MOE_TILE = 512
COMBINE_TILE = 256
ROUTER_PAD = LANES


def _cparams(sem):
    return pltpu.CompilerParams(dimension_semantics=sem, vmem_limit_bytes=VMEM_LIMIT_BYTES)


class _Tokens:
    def __init__(self, b_p, s_p, b_s, s_s):
        assert s_p % CHUNK == 0 and s_s % CHUNK == 0
        self.b_p, self.s_p, self.b_s, self.s_s = b_p, s_p, b_s, s_s
        self.n_p = b_p * s_p
        self.n = self.n_p + b_s * s_s
        self.n_chunks = self.n // CHUNK
        self.max_s = max(s_p, s_s)

    def batch_row(self, t0):
        return jnp.where(t0 < self.n_p, t0 // self.s_p, self.b_p + (t0 - self.n_p) // self.s_s)

    def pos(self, t0):
        return jnp.where(t0 < self.n_p, t0 % self.s_p, (t0 - self.n_p) % self.s_s)

    def seq_len(self, t0):
        return jnp.where(t0 < self.n_p, self.s_p, self.s_s)


def _mod_kernel(c_ref, w_ref, b_ref, o_ref):
    c = c_ref[...]
    act = c * jax.nn.sigmoid(c)
    o_ref[0] = jnp.dot(act, w_ref[0], precision=lax.Precision.HIGHEST,
                       preferred_element_type=F32) + b_ref[0]


def _modulation(c_all, w_mod, b_mod):
    depth, d, d6 = w_mod.shape
    rows = c_all.shape[0]
    return pl.pallas_call(
        _mod_kernel,
        grid=(depth, d6 // d),
        in_specs=[
            pl.BlockSpec((rows, d), lambda l, j: (0, 0)),
            pl.BlockSpec((1, d, d), lambda l, j: (l, 0, j)),
            pl.BlockSpec((1, 1, d), lambda l, j: (l, 0, j)),
        ],
        out_specs=pl.BlockSpec((1, rows, d), lambda l, j: (l, 0, j)),
        out_shape=jax.ShapeDtypeStruct((depth, rows, d6), F32),
        compiler_params=_cparams(("arbitrary", "arbitrary")),
        name="modulation",
    )(c_all, w_mod, b_mod.reshape(depth, 1, d6))


def _norm_modulate(x, g, shift, scale):
    y = x * lax.rsqrt(jnp.mean(x * x, axis=-1, keepdims=True) + NORM_EPS)
    return (y * g) * (1.0 + scale) + shift


N_SLABS = 4


def _qkv_kernel(x_ref, g_ref, sh_ref, sc_ref, w_ref, qg_ref, kg_ref, cos_ref, sin_ref, e_ref,
                *rest, kinds, rope, dils):
    n_out = 1 if dils is None else len(dils)
    o_refs, h_ref = rest[:n_out], rest[n_out]
    slab = None if dils is None else rest[n_out + 1]
    tm = x_ref.shape[0]
    hpg = HEADS_PER_GROUP_A // 2
    h_ref[...] = _norm_modulate(x_ref[...], g_ref[...], sh_ref[0], sc_ref[0]).astype(BF16)
    lane = lax.broadcasted_iota(jnp.int32, (1, LANES), 1)
    first_half = (lane % HEAD_DIM) < (HEAD_DIM // 2)
    n_strided = 0
    for c in range(len(kinds) // 2):
        acc = jnp.dot(h_ref[...], w_ref[:, c * 2 * LANES:(c + 1) * 2 * LANES],
                      preferred_element_type=F32)
        if kinds[2 * c] != "v":
            ss = jnp.dot((acc * acc).astype(BF16), e_ref[...], preferred_element_type=F32)
            acc = acc * lax.rsqrt(ss * (1.0 / HEAD_DIM) + NORM_EPS)
        for half in range(2):
            cb = 2 * c + half
            y = acc[:, half * LANES:(half + 1) * LANES]
            if kinds[cb] != "v":
                y = y * (qg_ref[...] if kinds[cb] == "q" else kg_ref[...])
                if rope:
                    partner = jnp.where(first_half, pltpu.roll(y, LANES - HEAD_DIM // 2, 1),
                                        pltpu.roll(y, HEAD_DIM // 2, 1))
                    y = y * cos_ref[...] + partner * sin_ref[...]
            if dils is None:
                o_refs[0][cb] = y.astype(BF16)
                continue
            ng = len(dils)
            kind, g, hp = cb // (ng * hpg), (cb % (ng * hpg)) // hpg, cb % hpg
            idx, dil = kind * hpg + hp, dils[g]
            if dil == 1:
                o_refs[g][idx, 0, 0] = y.astype(BF16)
            else:
                s = n_strided % N_SLABS
                n_strided += 1
                slab[s] = y
                for p in range(dil):
                    o_refs[g][idx, 0, p] = slab[s, pl.ds(p, tm // dil, stride=dil), :].astype(BF16)


def _qkv_call(tk, x, g, mod3, shift_j, scale_j, w, q_gain, k_gain, cos, sin, kinds, rope, dils=None):
    n, d = x.shape
    ncols = w.shape[1]
    ncb = ncols // LANES
    tm = ROW_TILE
    if dils is None:
        out_specs = pl.BlockSpec((ncb, tm, LANES), lambda i: (0, i, 0))
        out_shape = jax.ShapeDtypeStruct((ncb, n, LANES), BF16)
        scratch = [pltpu.VMEM((tm, d), BF16)]
    else:
        tpc = CHUNK // tm
        gcb = ncb // len(dils)
        out_specs = [pl.BlockSpec((gcb, 1, dil, tm // dil, LANES),
                                  lambda i: (0, i // tpc, 0, i % tpc, 0)) for dil in dils]
        out_shape = [jax.ShapeDtypeStruct((gcb, n // CHUNK, dil, CHUNK // dil, LANES), BF16)
                     for dil in dils]
        scratch = [pltpu.VMEM((tm, d), BF16), pltpu.VMEM((N_SLABS, tm, LANES), F32)]
    assert all(kinds[2 * c] == kinds[2 * c + 1] for c in range(ncb // 2))
    ones = np.kron(np.eye(2 * LANES // HEAD_DIM), np.ones((HEAD_DIM, HEAD_DIM))).astype(np.float32)
    e = jnp.asarray(ones, BF16)
    scale = HEAD_DIM ** -0.5
    qg = jnp.tile(q_gain.reshape(1, HEAD_DIM), (1, LANES // HEAD_DIM)) * scale
    kg = jnp.tile(k_gain.reshape(1, HEAD_DIM), (1, LANES // HEAD_DIM))
    brow = lambda i: tk.batch_row(i * tm)
    prow = lambda i: tk.pos(i * tm) // tm
    kern = functools.partial(_qkv_kernel, kinds=kinds, rope=rope, dils=dils)
    return pl.pallas_call(
        kern,
        grid=(n // tm,),
        in_specs=[
            pl.BlockSpec((tm, d), lambda i: (i, 0)),
            pl.BlockSpec((1, d), lambda i: (0, 0)),
            pl.BlockSpec((1, 1, d), lambda i: (brow(i), 0, shift_j)),
            pl.BlockSpec((1, 1, d), lambda i: (brow(i), 0, scale_j)),
            pl.BlockSpec((d, ncols), lambda i: (0, 0)),
            pl.BlockSpec((1, LANES), lambda i: (0, 0)),
            pl.BlockSpec((1, LANES), lambda i: (0, 0)),
            pl.BlockSpec((tm, LANES), lambda i: (prow(i), 0)),
            pl.BlockSpec((tm, LANES), lambda i: (prow(i), 0)),
            pl.BlockSpec((2 * LANES, 2 * LANES), lambda i: (0, 0)),
        ],
        out_specs=out_specs,
        out_shape=out_shape,
        scratch_shapes=scratch,
        compiler_params=_cparams(("arbitrary",)),
        name="norm_qkv",
    )(x, g, mod3, mod3, w, qg, kg, cos, sin, e)


def _rope_tables(max_s):
    half = HEAD_DIM // 2
    inv = ROPE_THETA ** (-np.arange(half, dtype=np.float64) / half)
    ang = np.arange(max_s, dtype=np.float64)[:, None] * inv[None, :]
    cos = np.cos(ang)
    sin = np.sin(ang)
    cos_head = np.concatenate([cos, cos], axis=1)
    sin_head = np.concatenate([-sin, sin], axis=1)
    reps = LANES // HEAD_DIM
    return (jnp.asarray(np.tile(cos_head, (1, reps)), F32),
            jnp.asarray(np.tile(sin_head, (1, reps)), F32))


def _band_kernel(q_ref, kp_ref, kc_ref, kn_ref, vp_ref, vc_ref, vn_ref, o_ref, lse_ref, *,
                 lc, chunks_p, chunks_per_seq_p):
    c = pl.program_id(0)
    i = pl.program_id(2)
    tq = BAND_TQ
    r = BAND_RADIUS
    kw = tq + 2 * r
    in_prompt = c < chunks_p
    cofs = jnp.where(in_prompt, c % chunks_per_seq_p, 0)
    seq_rows = jnp.where(in_prompt, chunks_per_seq_p, 1) * lc
    u0 = cofs * lc + i * tq
    row = lax.broadcasted_iota(jnp.int32, (tq, kw), 0)
    col = lax.broadcasted_iota(jnp.int32, (tq, kw), 1)
    uk = u0 - r + col
    valid = (jnp.abs(col - r - row) <= r) & (uk >= 0) & (uk < seq_rows)
    lane = lax.broadcasted_iota(jnp.int32, (1, LANES), 1)
    lo = lane < HEAD_DIM
    hi = jnp.logical_not(lo)
    nhp = q_ref.shape[0]
    qs, ks, vs = [], [], []
    for hp in range(nhp):
        q = q_ref[hp, 0, 0]
        k = jnp.concatenate([kp_ref[hp, 0, 0], kc_ref[hp, 0, 0], kn_ref[hp, 0, 0]], axis=0)
        v = jnp.concatenate([vp_ref[hp, 0, 0], vc_ref[hp, 0, 0], vn_ref[hp, 0, 0]], axis=0)
        qs += [jnp.where(lo, q, jnp.zeros_like(q)), jnp.where(hi, q, jnp.zeros_like(q))]
        ks += [k, k]
        vs += [v, v]
    s = lax.dot_general(jnp.stack(qs), jnp.stack(ks), (((2,), (2,)), ((0,), (0,))),
                        preferred_element_type=F32)
    s = jnp.where(valid[None], s, NEG_INF)
    m = jnp.max(s, axis=-1, keepdims=True)
    p = jnp.exp(s - m)
    den = jnp.sum(p, axis=-1, keepdims=True)
    o = lax.dot_general(p.astype(BF16), jnp.stack(vs), (((2,), (1,)), ((0,), (0,))),
                        preferred_element_type=F32) / den
    lse = m + jnp.log(den)
    for hp in range(nhp):
        o_ref[hp, 0, 0] = jnp.where(lo, o[2 * hp], o[2 * hp + 1]).astype(BF16)
        lse_ref[hp, 0, 0] = jnp.where(lo, lse[2 * hp], lse[2 * hp + 1])


def _band_call(tk, arr, q_blk, k_blk, v_blk, dil):
    nch = tk.n_chunks
    lc = CHUNK // dil
    tq = BAND_TQ
    r = BAND_RADIUS
    hpg = HEADS_PER_GROUP_A // 2
    per = tq // r
    nb = lc // r

    def cur(blk):
        return pl.BlockSpec((hpg, 1, 1, tq, LANES), lambda c, p, i: (blk, c, p, i, 0))

    def prev(blk):
        def imap(c, p, i):
            j = per * i - 1
            return (blk, jnp.where(j < 0, jnp.maximum(c - 1, 0), c), p, jnp.where(j < 0, nb - 1, j), 0)
        return pl.BlockSpec((hpg, 1, 1, r, LANES), imap)

    def nxt(blk):
        def imap(c, p, i):
            j = per * (i + 1)
            return (blk, jnp.where(j >= nb, jnp.minimum(c + 1, nch - 1), c), p,
                    jnp.where(j >= nb, 0, j), 0)
        return pl.BlockSpec((hpg, 1, 1, r, LANES), imap)

    kern = functools.partial(_band_kernel, lc=lc, chunks_p=tk.n_p // CHUNK,
                             chunks_per_seq_p=tk.s_p // CHUNK)
    out_spec = pl.BlockSpec((hpg, 1, 1, tq, LANES), lambda c, p, i: (0, c, p, i, 0))
    return pl.pallas_call(
        kern,
        grid=(nch, dil, lc // tq),
        in_specs=[cur(q_blk), prev(k_blk), cur(k_blk), nxt(k_blk), prev(v_blk), cur(v_blk), nxt(v_blk)],
        out_specs=[out_spec, out_spec],
        out_shape=[jax.ShapeDtypeStruct((hpg, nch, dil, lc, LANES), BF16),
                   jax.ShapeDtypeStruct((hpg, nch, dil, lc, LANES), F32)],
        compiler_params=_cparams(("arbitrary", "arbitrary", "arbitrary")),
        name="band_attention",
    )(arr, arr, arr, arr, arr, arr, arr)


def _dilated_attention(tk, qkv_groups):
    return [_band_call(tk, arr, 0, 1, 2, dil) for arr, dil in zip(qkv_groups, DILATIONS)]


def _na_kernel(q_ref, kp_ref, kc_ref, kn_ref, vp_ref, vc_ref, vn_ref, bias_ref, o_ref, kbuf, vbuf, *,
               rows_p, rows_s, blocks_p):
    bq = pl.program_id(1)
    tq = NA_TQ
    rows_per_blk = tq // GRID_W
    in_prompt = bq < blocks_p
    rows = jnp.where(in_prompt, rows_p, rows_s)
    blk_in_seq = jnp.where(in_prompt, bq % (rows_p // rows_per_blk),
                           (bq - blocks_p) % (rows_s // rows_per_blk))
    r0 = blk_in_seq * rows_per_blk
    kbuf[0 * tq:1 * tq] = kp_ref[0]
    kbuf[1 * tq:2 * tq] = kc_ref[0]
    kbuf[2 * tq:3 * tq] = kn_ref[0]
    vbuf[0 * tq:1 * tq] = vp_ref[0]
    vbuf[1 * tq:2 * tq] = vc_ref[0]
    vbuf[2 * tq:3 * tq] = vn_ref[0]
    lane = lax.broadcasted_iota(jnp.int32, (1, LANES), 1)
    lo = lane < HEAD_DIM
    hi = jnp.logical_not(lo)
    nkeys = NA_ROWS * GRID_W
    qs, ks, vs, bs = [], [], [], []
    for d in range(rows_per_blk):
        r = r0 + d
        rs = jnp.clip(r - NA_ROWS // 2, 0, rows - NA_ROWS)
        delta = r - rs
        off = pl.multiple_of((rs - (r0 - rows_per_blk)) * GRID_W, GRID_W)
        k = kbuf[pl.ds(off, nkeys), :]
        v = vbuf[pl.ds(off, nkeys), :]
        q = q_ref[0, d * GRID_W:(d + 1) * GRID_W, :]
        qs += [jnp.where(lo, q, jnp.zeros_like(q)), jnp.where(hi, q, jnp.zeros_like(q))]
        ks += [k, k]
        vs += [v, v]
        bs += [bias_ref[0, 0, delta], bias_ref[0, 1, delta]]
    s = lax.dot_general(jnp.stack(qs), jnp.stack(ks), (((2,), (2,)), ((0,), (0,))),
                        preferred_element_type=F32)
    s = s + jnp.stack(bs)
    m = jnp.max(s, axis=-1, keepdims=True)
    p = jnp.exp(s - m)
    den = jnp.sum(p, axis=-1, keepdims=True)
    o = lax.dot_general(p.astype(BF16), jnp.stack(vs), (((2,), (1,)), ((0,), (0,))),
                        preferred_element_type=F32) / den
    for d in range(rows_per_blk):
        o_ref[0, d * GRID_W:(d + 1) * GRID_W, :] = jnp.where(lo, o[2 * d], o[2 * d + 1]).astype(BF16)


def _na_bias_table(rpb):
    qc = np.arange(GRID_W)
    cstart = np.clip(qc - NA_COLS // 2, 0, GRID_W - NA_COLS)
    kc = np.arange(GRID_W)
    col_valid = (kc[None, :] >= cstart[:, None]) & (kc[None, :] < cstart[:, None] + NA_COLS)
    col_off = np.clip(kc[None, :] - qc[:, None] + NA_COLS - 1, 0, 2 * NA_COLS - 2)
    delta = np.arange(NA_ROWS)
    row_off = np.arange(NA_ROWS)[None, :] - delta[:, None] + NA_ROWS - 1
    t = rpb[:, row_off]
    t = t[:, :, :, col_off]
    t = jnp.where(jnp.asarray(col_valid)[None, None, None], t, NEG_INF)
    t = jnp.transpose(t, (0, 1, 3, 2, 4))
    h = rpb.shape[0]
    return t.reshape(h // 2, 2, NA_ROWS, GRID_W, NA_ROWS * GRID_W).astype(F32)


def _na_call(tk, qkv, bias):
    n = tk.n
    tq = NA_TQ
    nhp = N_HEADS_B // 2
    nblk = n // tq
    arr = qkv.reshape(3 * nhp, nblk, tq, LANES)
    nkeys = NA_ROWS * GRID_W

    arr3 = arr
    kern = functools.partial(_na_kernel, rows_p=tk.s_p // GRID_W, rows_s=tk.s_s // GRID_W,
                             blocks_p=tk.n_p // tq)

    def bspec(base, shift):
        return pl.BlockSpec((None, 1, tq, LANES),
                            lambda hp, b: (base + hp, jnp.clip(b + shift, 0, nblk - 1), 0, 0))

    return pl.pallas_call(
        kern,
        grid=(nhp, nblk),
        in_specs=[bspec(0, 0), bspec(nhp, -1), bspec(nhp, 0), bspec(nhp, 1),
                  bspec(2 * nhp, -1), bspec(2 * nhp, 0), bspec(2 * nhp, 1),
                  pl.BlockSpec((1, 2, NA_ROWS, GRID_W, nkeys), lambda hp, b: (hp, 0, 0, 0, 0))],
        out_specs=pl.BlockSpec((None, 1, tq, LANES), lambda hp, b: (hp, b, 0, 0)),
        out_shape=jax.ShapeDtypeStruct((nhp, nblk, tq, LANES), BF16),
        scratch_shapes=[pltpu.VMEM((3 * tq, LANES), BF16), pltpu.VMEM((3 * tq, LANES), BF16)],
        compiler_params=_cparams(("arbitrary", "arbitrary")),
        name="neighbourhood_attention",
    )(arr3, arr3, arr3, arr3, arr3, arr3, arr3, bias).reshape(nhp, n, LANES)


def _oproj_kernel(*refs, dils, n_tiles):
    n_groups = 1 if dils is None else len(dils)
    o_refs = refs[:n_groups]
    l_refs = refs[n_groups:2 * n_groups] if dils is not None else ()
    k = len(o_refs) + len(l_refs)
    (x_ref, wo_ref, g1_ref, ng_ref, sh_ref, sc_ref, wrh_ref, wrl_ref, br_ref,
     xo_ref, h_ref, lg_ref, m_ref) = refs[k:k + 13]
    slab = refs[k + 13] if dils is not None else None
    tm = x_ref.shape[0]
    n_strided = [0]

    def natural(ref, t, dil):
        if dil == 1:
            return ref[t, 0, 0].astype(F32)
        s = n_strided[0] % N_SLABS
        n_strided[0] += 1
        for p in range(dil):
            slab[s, pl.ds(p, tm // dil, stride=dil), :] = ref[t, 0, p].astype(F32)
        return slab[s]

    for t in range(n_tiles):
        if dils is None:
            m_ref[:, t * LANES:(t + 1) * LANES] = o_refs[0][t]
        else:
            ls = [natural(l, t, dil) for l, dil in zip(l_refs, dils)]
            mx = functools.reduce(jnp.maximum, ls)
            es = [jnp.exp(l - mx) for l in ls]
            den = functools.reduce(lambda a, b: a + b, es)
            num = functools.reduce(lambda a, b: a + b,
                                   [e * natural(o, t, dil) for e, o, dil in zip(es, o_refs, dils)])
            m_ref[:, t * LANES:(t + 1) * LANES] = (num / den).astype(BF16)
    mix = jnp.dot(m_ref[...], wo_ref[...], preferred_element_type=F32)
    x = x_ref[...] + g1_ref[0] * mix
    xo_ref[...] = x
    h = _norm_modulate(x, ng_ref[...], sh_ref[0], sc_ref[0])
    hi = h.astype(BF16)
    lo = (h - hi.astype(F32)).astype(BF16)
    tm, d = h.shape
    nsub = d // LANES
    for j in range(nsub):
        h_ref[pl.ds(j, tm, stride=nsub), :] = h[:, j * LANES:(j + 1) * LANES]
    lg = jnp.dot(hi, wrh_ref[...], preferred_element_type=F32)
    lg = lg + jnp.dot(lo, wrh_ref[...], preferred_element_type=F32)
    lg = lg + jnp.dot(hi, wrl_ref[...], preferred_element_type=F32)
    lg_ref[...] = lg + br_ref[...]


def _oproj_call(tk, os, ls, x, w_o, mod3, g1_j, ng, sh_j, sc_j, w_router, b_router, dils=None):
    n, d = x.shape
    tm = ROW_TILE
    n_tiles = os[0].shape[0]
    k_in = n_tiles * LANES
    if dils is None:
        tile_specs = [pl.BlockSpec((n_tiles, tm, LANES), lambda i: (0, i, 0))]
        scratch = [pltpu.VMEM((tm, k_in), BF16)]
    else:
        tpc = CHUNK // tm
        tile_specs = [pl.BlockSpec((n_tiles, 1, dil, tm // dil, LANES),
                                   lambda i: (0, i // tpc, 0, i % tpc, 0)) for dil in dils] * 2
        scratch = [pltpu.VMEM((tm, k_in), BF16), pltpu.VMEM((N_SLABS, tm, LANES), F32)]
    ne = w_router.shape[1]
    wr = jnp.zeros((d, ROUTER_PAD), F32).at[:, :ne].set(w_router)
    wr_hi = wr.astype(BF16)
    wr_lo = (wr - wr_hi.astype(F32)).astype(BF16)
    br = jnp.zeros((1, ROUTER_PAD), F32).at[0, :ne].set(b_router)
    brow = lambda i: tk.batch_row(i * tm)
    row_spec = pl.BlockSpec((tm, d), lambda i: (i, 0))
    mod_spec = lambda j: pl.BlockSpec((1, 1, d), lambda i: (brow(i), 0, j))
    kern = functools.partial(_oproj_kernel, dils=dils, n_tiles=n_tiles)
    return pl.pallas_call(
        kern,
        grid=(n // tm,),
        in_specs=tile_specs + [
            row_spec,
            pl.BlockSpec((k_in, d), lambda i: (0, 0)),
            mod_spec(g1_j),
            pl.BlockSpec((1, d), lambda i: (0, 0)),
            mod_spec(sh_j),
            mod_spec(sc_j),
            pl.BlockSpec((d, ROUTER_PAD), lambda i: (0, 0)),
            pl.BlockSpec((d, ROUTER_PAD), lambda i: (0, 0)),
            pl.BlockSpec((1, ROUTER_PAD), lambda i: (0, 0)),
        ],
        out_specs=[row_spec, pl.BlockSpec((tm * (d // LANES), LANES), lambda i: (i, 0)),
                   pl.BlockSpec((tm, ROUTER_PAD), lambda i: (i, 0))],
        out_shape=[jax.ShapeDtypeStruct((n, d), F32), jax.ShapeDtypeStruct((n * (d // LANES), LANES), F32),
                   jax.ShapeDtypeStruct((n, ROUTER_PAD), F32)],
        scratch_shapes=scratch,
        compiler_params=_cparams(("arbitrary",)),
        name="oproj_norm_router",
    )(*os, *ls, x, w_o, mod3, ng, mod3, mod3, wr_hi, wr_lo, br)


DMA_UNROLL = 16


def _moe_kernel(te_ref, nu_ref, gcur_ref, gnxt_ref, sprev_ref, h_hbm, wgu_ref, bgu_ref, wdn_ref,
                bdn_ref, y_hbm, xbuf, ybuf, acc_ref, wgu_b, wdn_b, sem_g, sem_s, *, d_ff, fchunk):
    t = pl.program_id(0)
    tm, d = acc_ref.shape
    nsub = d // LANES
    half = tm // 2
    n_used = nu_ref[0]
    n_chunks = d_ff // fchunk
    assert n_chunks == 4

    def row(ref, r):
        return ref.at[pl.ds(pl.multiple_of(r * nsub, nsub), nsub)]

    def gather_row(idx_ref, r, buf):
        pltpu.make_async_copy(row(h_hbm, idx_ref[0, 0, r]), row(xbuf.at[buf], r), sem_g.at[buf]).start()

    def scatter_row(idx_ref, r, buf):
        pltpu.make_async_copy(row(ybuf.at[buf], r), row(y_hbm, idx_ref[0, 0, r]), sem_s.at[buf]).start()

    def wait_gather_half(buf):
        pltpu.make_async_copy(h_hbm.at[pl.ds(0, half * nsub)], xbuf.at[buf, pl.ds(0, half * nsub)],
                              sem_g.at[buf]).wait()

    def wait_scatter_half(buf):
        pltpu.make_async_copy(ybuf.at[buf, pl.ds(0, half * nsub)], y_hbm.at[pl.ds(0, half * nsub)],
                              sem_s.at[buf]).wait()

    def rolled(issue):
        def body(i, carry):
            for j in range(DMA_UNROLL):
                issue(i * DMA_UNROLL + j)
            return carry
        lax.fori_loop(0, tm // DMA_UNROLL, body, 0)

    @pl.when(t == 0)
    def _():
        ybuf[...] = jnp.zeros_like(ybuf)
        spare = y_hbm.shape[0] - 2 * tm * nsub
        pltpu.make_async_copy(ybuf.at[0], y_hbm.at[pl.ds(spare, tm * nsub)], sem_s.at[0]).start()
        rolled(lambda r: gather_row(gcur_ref, r, 0))
        wait_gather_half(0)

    def step(s):
        o = 1 - s

        @pl.when((t == 0) | (te_ref[t] != te_ref[jnp.maximum(t - 1, 0)]))
        def _():
            rows = 128
            def cast(i, carry):
                r0 = pl.multiple_of(i * rows, rows)
                wgu_b[pl.ds(r0, rows), :] = wgu_ref[0, 0, pl.ds(r0, rows), :].astype(BF16)
                wdn_b[pl.ds(r0, rows), :] = wdn_ref[0, 0, pl.ds(r0, rows), :].astype(BF16)
                return carry
            lax.fori_loop(0, d // rows, cast, 0)

        wait_gather_half(s)
        x = jnp.concatenate([xbuf[s, pl.ds(j, tm, stride=nsub), :] for j in range(nsub)],
                            axis=1).astype(BF16)
        third = -(-2 * tm // 3)
        issues = [("g", r) for r in range(tm)] + [("s", r) for r in range(tm)]
        groups = [issues[:third], issues[third:2 * third], issues[2 * third:], []]
        for c in range(n_chunks):
            lo, hi = c * fchunk, (c + 1) * fchunk
            if c == 1:
                wait_scatter_half(s)
            if c == 2:
                wait_scatter_half(s)
            if c == 3:
                wait_gather_half(o)
            gate = jnp.dot(x, wgu_b[:, lo:hi], preferred_element_type=F32) + bgu_ref[0, 0, :, lo:hi]
            up = jnp.dot(x, wgu_b[:, d_ff + lo:d_ff + hi],
                         preferred_element_type=F32) + bgu_ref[0, 0, :, d_ff + lo:d_ff + hi]
            gate = jnp.minimum(gate, SWIGLU_LIMIT)
            up = jnp.clip(up, -SWIGLU_LIMIT, SWIGLU_LIMIT)
            a = ((up + 1.0) * gate * jax.nn.sigmoid(SWIGLU_ALPHA * gate)).astype(BF16)
            part = jnp.dot(a, wdn_b[lo:hi, :], preferred_element_type=F32)
            if c == 0:
                acc_ref[...] = part + bdn_ref[0, 0]
            else:
                acc_ref[...] += part
            for kind, r in groups[c]:
                if kind == "g":
                    gather_row(gnxt_ref, r, o)
                else:
                    scatter_row(sprev_ref, r, o)
        for j in range(nsub):
            ybuf[s, pl.ds(j, tm, stride=nsub), :] = acc_ref[:, j * LANES:(j + 1) * LANES]

    def drain(s):
        o = 1 - s
        wait_gather_half(s)
        rolled(lambda r: scatter_row(sprev_ref, r, o))
        for buf in (0, 1):
            wait_scatter_half(buf)
            wait_scatter_half(buf)

    for s in (0, 1):
        pl.when((t < n_used) & (t % 2 == s))(functools.partial(step, s))
        pl.when((t == n_used) & (t % 2 == s))(functools.partial(drain, s))


def _moe_call(h, route, w_gu, b_gu, w_dn, b_dn, layer):
    depth, ne, d, d_ff2 = w_gu.shape
    nsub = d // LANES
    n = h.shape[0] // nsub
    d_ff = d_ff2 // 2
    tm = MOE_TILE
    n_tiles = route["tile_expert"].shape[0]
    kern = functools.partial(_moe_kernel, d_ff=d_ff, fchunk=256)
    idx_spec = lambda shift: pl.BlockSpec(
        (1, 1, tm), lambda t, te, nu: (jnp.minimum(t + shift, n_tiles - 1), 0, 0),
        memory_space=pltpu.SMEM)
    grid_spec = pltpu.PrefetchScalarGridSpec(
        num_scalar_prefetch=2,
        grid=(n_tiles,),
        in_specs=[
            idx_spec(0), idx_spec(1), idx_spec(0),
            pl.BlockSpec(memory_space=pl.ANY),
            pl.BlockSpec((1, 1, d, d_ff2), lambda t, te, nu: (layer, te[t], 0, 0)),
            pl.BlockSpec((1, 1, 1, d_ff2), lambda t, te, nu: (layer, te[t], 0, 0)),
            pl.BlockSpec((1, 1, d_ff, d), lambda t, te, nu: (layer, te[t], 0, 0)),
            pl.BlockSpec((1, 1, 1, d), lambda t, te, nu: (layer, te[t], 0, 0)),
        ],
        out_specs=pl.BlockSpec(memory_space=pl.ANY),
        scratch_shapes=[pltpu.VMEM((2, tm * nsub, LANES), F32), pltpu.VMEM((2, tm * nsub, LANES), F32),
                        pltpu.VMEM((tm, d), F32),
                        pltpu.VMEM((d, d_ff2), BF16), pltpu.VMEM((d_ff, d), BF16),
                        pltpu.SemaphoreType.DMA((2,)), pltpu.SemaphoreType.DMA((2,))],
    )
    return pl.pallas_call(
        kern,
        grid_spec=grid_spec,
        out_shape=jax.ShapeDtypeStruct(((TOP_K * n + 2 * tm) * nsub, LANES), F32),
        compiler_params=_cparams(("arbitrary",)),
        name="moe_experts",
    )(route["tile_expert"], route["n_used"], route["gsrc"], route["gsrc"], route["sdst_prev"], h,
      w_gu, b_gu.reshape(depth, ne, 1, d_ff2), w_dn, b_dn.reshape(depth, ne, 1, d))


def _route(logits, n):
    ne = N_EXPERTS
    tm = MOE_TILE
    top_val, top_idx = lax.top_k(logits[:, :ne], TOP_K)
    gates = jax.nn.softmax(top_val, axis=-1)
    e_flat = top_idx.reshape(-1).astype(jnp.int32)
    nk = n * TOP_K
    counts = jnp.sum((e_flat[:, None] == jnp.arange(ne, dtype=jnp.int32)[None, :]).astype(jnp.int32),
                     axis=0)
    padded = (counts + tm - 1) // tm * tm
    pad_ends = jnp.cumsum(padded)
    n_tiles = nk // tm + ne
    cap = n_tiles * tm
    tile_start = jnp.arange(n_tiles, dtype=jnp.int32) * tm
    tile_expert = jnp.minimum(jnp.sum((pad_ends[None, :] <= tile_start[:, None]).astype(jnp.int32), axis=1),
                              ne - 1).astype(jnp.int32)
    n_used = (pad_ends[-1] // tm).astype(jnp.int32).reshape(1)
    idx_bits = int(np.ceil(np.log2(nk + tm)))
    pad_base = (1 << idx_bits) - tm
    real_keys = (e_flat << idx_bits) + jnp.arange(nk, dtype=jnp.int32)
    lane = jnp.arange(tm, dtype=jnp.int32)[None, :]
    expert = jnp.arange(ne, dtype=jnp.int32)[:, None]
    pad_keys = jnp.where(lane < (padded - counts)[:, None], (expert << idx_bits) + pad_base + lane,
                         jnp.iinfo(jnp.int32).max)
    keys = jnp.concatenate([real_keys, pad_keys.reshape(-1)])
    low = lax.sort(keys, is_stable=False) & ((1 << idx_bits) - 1)
    valid = low < pad_base
    slot = jnp.arange(cap, dtype=jnp.int32)
    gsrc = jnp.where(valid, low // TOP_K, 0).astype(jnp.int32)
    spare = nk + ((slot // tm) % 2) * tm + slot % tm
    sdst = jnp.where(valid, (low % TOP_K) * n + low // TOP_K, spare).astype(jnp.int32)
    sdst = sdst.reshape(n_tiles, 1, tm)
    before_first = (nk + tm + jnp.arange(tm, dtype=jnp.int32)).reshape(1, 1, tm)
    sdst_prev = jnp.concatenate([before_first, sdst[:-1]], axis=0)
    return dict(gates=gates, tile_expert=tile_expert, n_used=n_used,
                gsrc=gsrc.reshape(n_tiles, 1, tm), sdst_prev=sdst_prev)


def _combine_kernel(x_ref, y0_ref, y1_ref, y2_ref, y3_ref, gt_ref, g2_ref, o_ref):
    tm, d = x_ref.shape
    nsub = d // LANES
    g = gt_ref[...]
    for j in range(nsub):
        y = None
        for k, y_ref in enumerate((y0_ref, y1_ref, y2_ref, y3_ref)):
            term = g[:, k:k + 1] * y_ref[pl.ds(j, tm, stride=nsub), :]
            y = term if y is None else y + term
        cols = slice(j * LANES, (j + 1) * LANES)
        o_ref[:, cols] = x_ref[:, cols] + g2_ref[0, :, cols] * y


def _combine_call(tk, x, ycomb, gates, mod3, g2_j):
    n, d = x.shape
    nsub = d // LANES
    tm = COMBINE_TILE
    nt = n // tm
    brow = lambda i: tk.batch_row(i * tm)
    row_spec = pl.BlockSpec((tm, d), lambda i: (i, 0))
    y_spec = lambda k: pl.BlockSpec((tm * nsub, LANES), lambda i: (k * nt + i, 0))
    return pl.pallas_call(
        _combine_kernel,
        grid=(nt,),
        in_specs=[row_spec, y_spec(0), y_spec(1), y_spec(2), y_spec(3),
                  pl.BlockSpec((tm, TOP_K), lambda i: (i, 0)),
                  pl.BlockSpec((1, 1, d), lambda i: (brow(i), 0, g2_j))],
        out_specs=row_spec,
        out_shape=jax.ShapeDtypeStruct((n, d), F32),
        compiler_params=_cparams(("arbitrary",)),
        name="moe_combine",
    )(x, ycomb, ycomb, ycomb, ycomb, gates, mod3)


def _moe_experts(tk, h, logits, w_gu, b_gu, w_dn, b_dn, layer):
    route = _route(logits, logits.shape[0])
    return _moe_call(h, route, w_gu, b_gu, w_dn, b_dn, layer), route["gates"]


@jax.jit
def _forward(x_prompt, x_sample, c_prompt, c_sample, norm_g, w_mod, b_mod, a_w_qkv, a_q_gain,
             a_k_gain, a_w_o, b_w_qkv, b_q_gain, b_k_gain, b_rpb, b_w_o, w_router, b_router,
             w_gu, b_gu, w_dn, b_dn):
    b_p, s_p, d = x_prompt.shape
    b_s, s_s, _ = x_sample.shape
    tk = _Tokens(b_p, s_p, b_s, s_s)
    depth = w_mod.shape[0]
    x = jnp.concatenate([x_prompt.reshape(-1, d), x_sample.reshape(-1, d)], axis=0)

    nb = b_p + b_s
    rows = -(-nb // 8) * 8
    c_all = jnp.zeros((rows, d), F32).at[:nb].set(jnp.concatenate([c_prompt, c_sample], axis=0))
    mods = _modulation(c_all, w_mod, b_mod)

    cos, sin = _rope_tables(tk.max_s)
    kinds_a = tuple("qkv"[cb // (len(DILATIONS) * HEADS_PER_GROUP_A // 2)]
                    for cb in range(3 * len(DILATIONS) * HEADS_PER_GROUP_A // 2))
    kinds_b = tuple("qkv"[cb // (N_HEADS_B // 2)] for cb in range(3 * N_HEADS_B // 2))

    for layer in range(depth):
        mod3 = mods[layer].reshape(rows, 1, 6 * d)
        j = layer // 2
        if layer % 2 == 0:
            qkv = _qkv_call(tk, x, norm_g[layer, 0].reshape(1, d), mod3, 0, 1,
                            a_w_qkv[j].astype(BF16), a_q_gain[j], a_k_gain[j], cos, sin, kinds_a, True,
                            dils=DILATIONS)
            groups = _dilated_attention(tk, qkv)
            os = [g[0] for g in groups]
            ls = [g[1] for g in groups]
            w_o = a_w_o[j]
            dils = DILATIONS
        else:
            dils = None
            qkv = _qkv_call(tk, x, norm_g[layer, 0].reshape(1, d), mod3, 0, 1,
                            b_w_qkv[j].astype(BF16), b_q_gain[j], b_k_gain[j], cos, sin, kinds_b, False)
            os = [_na_call(tk, qkv, _na_bias_table(b_rpb[j]))]
            ls = []
            w_o = b_w_o[j]
        x, hff, logits = _oproj_call(tk, os, ls, x, w_o.astype(BF16), mod3, 2,
                                     norm_g[layer, 1].reshape(1, d), 3, 4,
                                     w_router[layer], b_router[layer], dils=dils)
        ycomb, gates = _moe_experts(tk, hff, logits, w_gu, b_gu, w_dn, b_dn, layer)
        x = _combine_call(tk, x, ycomb, gates, mod3, 5)

    y_prompt = x[:tk.n_p].reshape(b_p, s_p, d)
    y_sample = x[tk.n_p:].reshape(b_s, s_s, d)
    return y_prompt, y_sample


def kernel(x_prompt, x_sample, c_prompt, c_sample, norm_g, w_mod, b_mod, a_w_qkv, a_q_gain, a_k_gain,
           a_w_o, b_w_qkv, b_q_gain, b_k_gain, b_rpb, b_w_o, w_router, b_router, w_gu, b_gu, w_dn, b_dn):
    return _forward(x_prompt, x_sample, c_prompt, c_sample, norm_g, w_mod, b_mod, a_w_qkv, a_q_gain,
                    a_k_gain, a_w_o, b_w_qkv, b_q_gain, b_k_gain, b_rpb, b_w_o, w_router, b_router,
                    w_gu, b_gu, w_dn, b_dn)
```

```python
import functools

import numpy as np
import jax
import jax.numpy as jnp
from jax import lax
from jax.experimental import pallas as pl
from jax.experimental.pallas import tpu as pltpu

F32 = jnp.float32
BF16 = jnp.bfloat16

HEAD_DIM = 64
ROPE_THETA = 10000.0
NORM_EPS = 1e-6
NEG_INF = -1e30
DILATIONS = (1, 4, 16)
BAND_RADIUS = 64
HEADS_PER_GROUP_A = 8
N_HEADS_B = 16
GRID_W = 64
NA_ROWS = 8
NA_COLS = 16
N_EXPERTS = 32
TOP_K = 4
SWIGLU_LIMIT = 7.0
SWIGLU_ALPHA = 1.702

LANES = 128
VMEM_LIMIT_BYTES = 56 * 1024 * 1024

CHUNK = 2048
ROW_TILE = 512
BAND_TQ = 128
NA_TQ = 512
MOE_TILE = 512
COMBINE_TILE = 256
ROUTER_PAD = LANES


def _cparams(sem):
    return pltpu.CompilerParams(dimension_semantics=sem, vmem_limit_bytes=VMEM_LIMIT_BYTES)


class _Tokens:
    def __init__(self, b_p, s_p, b_s, s_s):
        assert s_p % CHUNK == 0 and s_s % CHUNK == 0
        self.b_p, self.s_p, self.b_s, self.s_s = b_p, s_p, b_s, s_s
        self.n_p = b_p * s_p
        self.n = self.n_p + b_s * s_s
        self.n_chunks = self.n // CHUNK
        self.max_s = max(s_p, s_s)

    def batch_row(self, t0):
        return jnp.where(t0 < self.n_p, t0 // self.s_p, self.b_p + (t0 - self.n_p) // self.s_s)

    def pos(self, t0):
        return jnp.where(t0 < self.n_p, t0 % self.s_p, (t0 - self.n_p) % self.s_s)

    def seq_len(self, t0):
        return jnp.where(t0 < self.n_p, self.s_p, self.s_s)


def _mod_kernel(c_ref, w_ref, b_ref, o_ref):
    c = c_ref[...]
    act = c * jax.nn.sigmoid(c)
    o_ref[0] = jnp.dot(act, w_ref[0], precision=lax.Precision.HIGHEST,
                       preferred_element_type=F32) + b_ref[0]


def _modulation(c_all, w_mod, b_mod):
    depth, d, d6 = w_mod.shape
    rows = c_all.shape[0]
    return pl.pallas_call(
        _mod_kernel,
        grid=(depth, d6 // d),
        in_specs=[
            pl.BlockSpec((rows, d), lambda l, j: (0, 0)),
            pl.BlockSpec((1, d, d), lambda l, j: (l, 0, j)),
            pl.BlockSpec((1, 1, d), lambda l, j: (l, 0, j)),
        ],
        out_specs=pl.BlockSpec((1, rows, d), lambda l, j: (l, 0, j)),
        out_shape=jax.ShapeDtypeStruct((depth, rows, d6), F32),
        compiler_params=_cparams(("arbitrary", "arbitrary")),
        name="modulation",
    )(c_all, w_mod, b_mod.reshape(depth, 1, d6))


def _norm_modulate(x, g, shift, scale):
    y = x * lax.rsqrt(jnp.mean(x * x, axis=-1, keepdims=True) + NORM_EPS)
    return (y * g) * (1.0 + scale) + shift


N_SLABS = 4


def _qkv_kernel(x_ref, g_ref, sh_ref, sc_ref, w_ref, qg_ref, kg_ref, cos_ref, sin_ref, e_ref,
                *rest, kinds, rope, dils):
    n_out = 1 if dils is None else len(dils)
    o_refs, h_ref = rest[:n_out], rest[n_out]
    slab = None if dils is None else rest[n_out + 1]
    tm = x_ref.shape[0]
    hpg = HEADS_PER_GROUP_A // 2
    h_ref[...] = _norm_modulate(x_ref[...], g_ref[...], sh_ref[0], sc_ref[0]).astype(BF16)
    lane = lax.broadcasted_iota(jnp.int32, (1, LANES), 1)
    first_half = (lane % HEAD_DIM) < (HEAD_DIM // 2)
    n_strided = 0
    wide = None
    for c in range(len(kinds) // 2):
        if c % 2 == 0:
            wide = jnp.dot(h_ref[...], w_ref[:, c * 2 * LANES:(c + 2) * 2 * LANES],
                           preferred_element_type=F32)
        acc = wide[:, (c % 2) * 2 * LANES:(c % 2 + 1) * 2 * LANES]
        if kinds[2 * c] != "v":
            ss = jnp.dot((acc * acc).astype(BF16), e_ref[...], preferred_element_type=F32)
            acc = acc * lax.rsqrt(ss * (1.0 / HEAD_DIM) + NORM_EPS)
        for half in range(2):
            cb = 2 * c + half
            y = acc[:, half * LANES:(half + 1) * LANES]
            if kinds[cb] != "v":
                y = y * (qg_ref[...] if kinds[cb] == "q" else kg_ref[...])
                if rope:
                    partner = jnp.where(first_half, pltpu.roll(y, LANES - HEAD_DIM // 2, 1),
                                        pltpu.roll(y, HEAD_DIM // 2, 1))
                    y = y * cos_ref[...] + partner * sin_ref[...]
            if dils is None:
                o_refs[0][cb] = y.astype(BF16)
                continue
            ng = len(dils)
            kind, g, hp = cb // (ng * hpg), (cb % (ng * hpg)) // hpg, cb % hpg
            idx, dil = kind * hpg + hp, dils[g]
            if dil == 1:
                o_refs[g][idx, 0, 0] = y.astype(BF16)
            else:
                s = n_strided % N_SLABS
                n_strided += 1
                slab[s] = y
                for p in range(dil):
                    o_refs[g][idx, 0, p] = slab[s, pl.ds(p, tm // dil, stride=dil), :].astype(BF16)


def _qkv_call(tk, x, g, mod3, shift_j, scale_j, w, q_gain, k_gain, cos, sin, kinds, rope, dils=None):
    n, d = x.shape
    ncols = w.shape[1]
    ncb = ncols // LANES
    tm = ROW_TILE
    if dils is None:
        out_specs = pl.BlockSpec((ncb, tm, LANES), lambda i: (0, i, 0))
        out_shape = jax.ShapeDtypeStruct((ncb, n, LANES), BF16)
        scratch = [pltpu.VMEM((tm, d), BF16)]
    else:
        tpc = CHUNK // tm
        gcb = ncb // len(dils)
        out_specs = [pl.BlockSpec((gcb, 1, dil, tm // dil, LANES),
                                  lambda i: (0, i // tpc, 0, i % tpc, 0)) for dil in dils]
        out_shape = [jax.ShapeDtypeStruct((gcb, n // CHUNK, dil, CHUNK // dil, LANES), BF16)
                     for dil in dils]
        scratch = [pltpu.VMEM((tm, d), BF16), pltpu.VMEM((N_SLABS, tm, LANES), F32)]
    assert all(kinds[2 * c] == kinds[2 * c + 1] for c in range(ncb // 2))
    ones = np.kron(np.eye(2 * LANES // HEAD_DIM), np.ones((HEAD_DIM, HEAD_DIM))).astype(np.float32)
    e = jnp.asarray(ones, BF16)
    scale = HEAD_DIM ** -0.5
    qg = jnp.tile(q_gain.reshape(1, HEAD_DIM), (1, LANES // HEAD_DIM)) * scale
    kg = jnp.tile(k_gain.reshape(1, HEAD_DIM), (1, LANES // HEAD_DIM))
    brow = lambda i: tk.batch_row(i * tm)
    prow = lambda i: tk.pos(i * tm) // tm
    kern = functools.partial(_qkv_kernel, kinds=kinds, rope=rope, dils=dils)
    return pl.pallas_call(
        kern,
        grid=(n // tm,),
        in_specs=[
            pl.BlockSpec((tm, d), lambda i: (i, 0)),
            pl.BlockSpec((1, d), lambda i: (0, 0)),
            pl.BlockSpec((1, 1, d), lambda i: (brow(i), 0, shift_j)),
            pl.BlockSpec((1, 1, d), lambda i: (brow(i), 0, scale_j)),
            pl.BlockSpec((d, ncols), lambda i: (0, 0)),
            pl.BlockSpec((1, LANES), lambda i: (0, 0)),
            pl.BlockSpec((1, LANES), lambda i: (0, 0)),
            pl.BlockSpec((tm, LANES), lambda i: (prow(i), 0)),
            pl.BlockSpec((tm, LANES), lambda i: (prow(i), 0)),
            pl.BlockSpec((2 * LANES, 2 * LANES), lambda i: (0, 0)),
        ],
        out_specs=out_specs,
        out_shape=out_shape,
        scratch_shapes=scratch,
        compiler_params=_cparams(("arbitrary",)),
        name="norm_qkv",
    )(x, g, mod3, mod3, w, qg, kg, cos, sin, e)


def _rope_tables(max_s):
    half = HEAD_DIM // 2
    inv = ROPE_THETA ** (-np.arange(half, dtype=np.float64) / half)
    ang = np.arange(max_s, dtype=np.float64)[:, None] * inv[None, :]
    cos = np.cos(ang)
    sin = np.sin(ang)
    cos_head = np.concatenate([cos, cos], axis=1)
    sin_head = np.concatenate([-sin, sin], axis=1)
    reps = LANES // HEAD_DIM
    return (jnp.asarray(np.tile(cos_head, (1, reps)), F32),
            jnp.asarray(np.tile(sin_head, (1, reps)), F32))


def _band_kernel(q_ref, kp_ref, kc_ref, kn_ref, vp_ref, vc_ref, vn_ref, o_ref, lse_ref, *,
                 lc, chunks_p, chunks_per_seq_p):
    c = pl.program_id(0)
    i = pl.program_id(2)
    tq = BAND_TQ
    r = BAND_RADIUS
    kw = tq + 2 * r
    in_prompt = c < chunks_p
    cofs = jnp.where(in_prompt, c % chunks_per_seq_p, 0)
    seq_rows = jnp.where(in_prompt, chunks_per_seq_p, 1) * lc
    u0 = cofs * lc + i * tq
    row = lax.broadcasted_iota(jnp.int32, (tq, kw), 0)
    col = lax.broadcasted_iota(jnp.int32, (tq, kw), 1)
    uk = u0 - r + col
    valid = (jnp.abs(col - r - row) <= r) & (uk >= 0) & (uk < seq_rows)
    lane = lax.broadcasted_iota(jnp.int32, (1, LANES), 1)
    lo = lane < HEAD_DIM
    hi = jnp.logical_not(lo)
    nhp = q_ref.shape[0]
    qs, ks, vs = [], [], []
    for hp in range(nhp):
        q = q_ref[hp, 0, 0]
        k = jnp.concatenate([kp_ref[hp, 0, 0], kc_ref[hp, 0, 0], kn_ref[hp, 0, 0]], axis=0)
        v = jnp.concatenate([vp_ref[hp, 0, 0], vc_ref[hp, 0, 0], vn_ref[hp, 0, 0]], axis=0)
        qs += [jnp.where(lo, q, jnp.zeros_like(q)), jnp.where(hi, q, jnp.zeros_like(q))]
        ks += [k, k]
        vs += [v, v]
    s = lax.dot_general(jnp.stack(qs), jnp.stack(ks), (((2,), (2,)), ((0,), (0,))),
                        preferred_element_type=F32)
    s = jnp.where(valid[None], s, NEG_INF)
    m = jnp.max(s, axis=-1, keepdims=True)
    p = jnp.exp(s - m)
    den = jnp.sum(p, axis=-1, keepdims=True)
    o = lax.dot_general(p.astype(BF16), jnp.stack(vs), (((2,), (1,)), ((0,), (0,))),
                        preferred_element_type=F32) / den
    lse = m + jnp.log(den)
    for hp in range(nhp):
        o_ref[hp, 0, 0] = jnp.where(lo, o[2 * hp], o[2 * hp + 1]).astype(BF16)
        lse_ref[hp, 0, 0] = jnp.where(lo, lse[2 * hp], lse[2 * hp + 1])


def _band_call(tk, arr, q_blk, k_blk, v_blk, dil):
    nch = tk.n_chunks
    lc = CHUNK // dil
    tq = BAND_TQ
    r = BAND_RADIUS
    hpg = HEADS_PER_GROUP_A // 2
    per = tq // r
    nb = lc // r

    def cur(blk):
        return pl.BlockSpec((hpg, 1, 1, tq, LANES), lambda c, p, i: (blk, c, p, i, 0))

    def prev(blk):
        def imap(c, p, i):
            j = per * i - 1
            return (blk, jnp.where(j < 0, jnp.maximum(c - 1, 0), c), p, jnp.where(j < 0, nb - 1, j), 0)
        return pl.BlockSpec((hpg, 1, 1, r, LANES), imap)

    def nxt(blk):
        def imap(c, p, i):
            j = per * (i + 1)
            return (blk, jnp.where(j >= nb, jnp.minimum(c + 1, nch - 1), c), p,
                    jnp.where(j >= nb, 0, j), 0)
        return pl.BlockSpec((hpg, 1, 1, r, LANES), imap)

    kern = functools.partial(_band_kernel, lc=lc, chunks_p=tk.n_p // CHUNK,
                             chunks_per_seq_p=tk.s_p // CHUNK)
    out_spec = pl.BlockSpec((hpg, 1, 1, tq, LANES), lambda c, p, i: (0, c, p, i, 0))
    return pl.pallas_call(
        kern,
        grid=(nch, dil, lc // tq),
        in_specs=[cur(q_blk), prev(k_blk), cur(k_blk), nxt(k_blk), prev(v_blk), cur(v_blk), nxt(v_blk)],
        out_specs=[out_spec, out_spec],
        out_shape=[jax.ShapeDtypeStruct((hpg, nch, dil, lc, LANES), BF16),
                   jax.ShapeDtypeStruct((hpg, nch, dil, lc, LANES), F32)],
        compiler_params=_cparams(("arbitrary", "arbitrary", "arbitrary")),
        name="band_attention",
    )(arr, arr, arr, arr, arr, arr, arr)


def _dilated_attention(tk, qkv_groups):
    return [_band_call(tk, arr, 0, 1, 2, dil) for arr, dil in zip(qkv_groups, DILATIONS)]


def _na_kernel(q_ref, kp_ref, kc_ref, kn_ref, vp_ref, vc_ref, vn_ref, bias_ref, o_ref, kbuf, vbuf, *,
               rows_p, rows_s, blocks_p):
    bq = pl.program_id(1)
    tq = NA_TQ
    rows_per_blk = tq // GRID_W
    in_prompt = bq < blocks_p
    rows = jnp.where(in_prompt, rows_p, rows_s)
    blk_in_seq = jnp.where(in_prompt, bq % (rows_p // rows_per_blk),
                           (bq - blocks_p) % (rows_s // rows_per_blk))
    r0 = blk_in_seq * rows_per_blk
    kbuf[0 * tq:1 * tq] = kp_ref[0]
    kbuf[1 * tq:2 * tq] = kc_ref[0]
    kbuf[2 * tq:3 * tq] = kn_ref[0]
    vbuf[0 * tq:1 * tq] = vp_ref[0]
    vbuf[1 * tq:2 * tq] = vc_ref[0]
    vbuf[2 * tq:3 * tq] = vn_ref[0]
    lane = lax.broadcasted_iota(jnp.int32, (1, LANES), 1)
    lo = lane < HEAD_DIM
    hi = jnp.logical_not(lo)
    nkeys = NA_ROWS * GRID_W
    qs, ks, vs, bs = [], [], [], []
    for d in range(rows_per_blk):
        r = r0 + d
        rs = jnp.clip(r - NA_ROWS // 2, 0, rows - NA_ROWS)
        delta = r - rs
        off = pl.multiple_of((rs - (r0 - rows_per_blk)) * GRID_W, GRID_W)
        k = kbuf[pl.ds(off, nkeys), :]
        v = vbuf[pl.ds(off, nkeys), :]
        q = q_ref[0, d * GRID_W:(d + 1) * GRID_W, :]
        qs += [jnp.where(lo, q, jnp.zeros_like(q)), jnp.where(hi, q, jnp.zeros_like(q))]
        ks += [k, k]
        vs += [v, v]
        bs += [bias_ref[0, 0, delta], bias_ref[0, 1, delta]]
    s = lax.dot_general(jnp.stack(qs), jnp.stack(ks), (((2,), (2,)), ((0,), (0,))),
                        preferred_element_type=F32)
    s = s + jnp.stack(bs)
    m = jnp.max(s, axis=-1, keepdims=True)
    p = jnp.exp(s - m)
    den = jnp.sum(p, axis=-1, keepdims=True)
    o = lax.dot_general(p.astype(BF16), jnp.stack(vs), (((2,), (1,)), ((0,), (0,))),
                        preferred_element_type=F32) / den
    for d in range(rows_per_blk):
        o_ref[0, d * GRID_W:(d + 1) * GRID_W, :] = jnp.where(lo, o[2 * d], o[2 * d + 1]).astype(BF16)


def _na_bias_table(rpb):
    qc = np.arange(GRID_W)
    cstart = np.clip(qc - NA_COLS // 2, 0, GRID_W - NA_COLS)
    kc = np.arange(GRID_W)
    col_valid = (kc[None, :] >= cstart[:, None]) & (kc[None, :] < cstart[:, None] + NA_COLS)
    col_off = np.clip(kc[None, :] - qc[:, None] + NA_COLS - 1, 0, 2 * NA_COLS - 2)
    delta = np.arange(NA_ROWS)
    row_off = np.arange(NA_ROWS)[None, :] - delta[:, None] + NA_ROWS - 1
    t = rpb[:, row_off]
    t = t[:, :, :, col_off]
    t = jnp.where(jnp.asarray(col_valid)[None, None, None], t, NEG_INF)
    t = jnp.transpose(t, (0, 1, 3, 2, 4))
    h = rpb.shape[0]
    return t.reshape(h // 2, 2, NA_ROWS, GRID_W, NA_ROWS * GRID_W).astype(F32)


def _na_call(tk, qkv, bias):
    n = tk.n
    tq = NA_TQ
    nhp = N_HEADS_B // 2
    nblk = n // tq
    arr = qkv.reshape(3 * nhp, nblk, tq, LANES)
    nkeys = NA_ROWS * GRID_W

    arr3 = arr
    kern = functools.partial(_na_kernel, rows_p=tk.s_p // GRID_W, rows_s=tk.s_s // GRID_W,
                             blocks_p=tk.n_p // tq)

    def bspec(base, shift):
        return pl.BlockSpec((None, 1, tq, LANES),
                            lambda hp, b: (base + hp, jnp.clip(b + shift, 0, nblk - 1), 0, 0))

    return pl.pallas_call(
        kern,
        grid=(nhp, nblk),
        in_specs=[bspec(0, 0), bspec(nhp, -1), bspec(nhp, 0), bspec(nhp, 1),
                  bspec(2 * nhp, -1), bspec(2 * nhp, 0), bspec(2 * nhp, 1),
                  pl.BlockSpec((1, 2, NA_ROWS, GRID_W, nkeys), lambda hp, b: (hp, 0, 0, 0, 0))],
        out_specs=pl.BlockSpec((None, 1, tq, LANES), lambda hp, b: (hp, b, 0, 0)),
        out_shape=jax.ShapeDtypeStruct((nhp, nblk, tq, LANES), BF16),
        scratch_shapes=[pltpu.VMEM((3 * tq, LANES), BF16), pltpu.VMEM((3 * tq, LANES), BF16)],
        compiler_params=_cparams(("arbitrary", "arbitrary")),
        name="neighbourhood_attention",
    )(arr3, arr3, arr3, arr3, arr3, arr3, arr3, bias).reshape(nhp, n, LANES)


def _oproj_kernel(*refs, dils, n_tiles):
    n_groups = 1 if dils is None else len(dils)
    o_refs = refs[:n_groups]
    l_refs = refs[n_groups:2 * n_groups] if dils is not None else ()
    k = len(o_refs) + len(l_refs)
    (x_ref, wo_ref, g1_ref, ng_ref, sh_ref, sc_ref, wrh_ref, wrl_ref, br_ref,
     xo_ref, h_ref, lg_ref, m_ref) = refs[k:k + 13]
    slab = refs[k + 13] if dils is not None else None
    tm = x_ref.shape[0]
    n_strided = [0]

    def natural(ref, t, dil):
        if dil == 1:
            return ref[t, 0, 0].astype(F32)
        s = n_strided[0] % N_SLABS
        n_strided[0] += 1
        for p in range(dil):
            slab[s, pl.ds(p, tm // dil, stride=dil), :] = ref[t, 0, p].astype(F32)
        return slab[s]

    for t in range(n_tiles):
        if dils is None:
            m_ref[:, t * LANES:(t + 1) * LANES] = o_refs[0][t]
        else:
            ls = [natural(l, t, dil) for l, dil in zip(l_refs, dils)]
            mx = functools.reduce(jnp.maximum, ls)
            es = [jnp.exp(l - mx) for l in ls]
            den = functools.reduce(lambda a, b: a + b, es)
            num = functools.reduce(lambda a, b: a + b,
                                   [e * natural(o, t, dil) for e, o, dil in zip(es, o_refs, dils)])
            m_ref[:, t * LANES:(t + 1) * LANES] = (num / den).astype(BF16)
    mix = jnp.dot(m_ref[...], wo_ref[...], preferred_element_type=F32)
    x = x_ref[...] + g1_ref[0] * mix
    xo_ref[...] = x
    h = _norm_modulate(x, ng_ref[...], sh_ref[0], sc_ref[0])
    hi = h.astype(BF16)
    lo = (h - hi.astype(F32)).astype(BF16)
    tm, d = h.shape
    nsub = d // LANES
    for j in range(nsub):
        h_ref[pl.ds(j, tm, stride=nsub), :] = h[:, j * LANES:(j + 1) * LANES]
    lg = jnp.dot(hi, wrh_ref[...], preferred_element_type=F32)
    lg = lg + jnp.dot(lo, wrh_ref[...], preferred_element_type=F32)
    lg = lg + jnp.dot(hi, wrl_ref[...], preferred_element_type=F32)
    lg_ref[...] = lg + br_ref[...]


def _oproj_call(tk, os, ls, x, w_o, mod3, g1_j, ng, sh_j, sc_j, w_router, b_router, dils=None):
    n, d = x.shape
    tm = ROW_TILE
    n_tiles = os[0].shape[0]
    k_in = n_tiles * LANES
    if dils is None:
        tile_specs = [pl.BlockSpec((n_tiles, tm, LANES), lambda i: (0, i, 0))]
        scratch = [pltpu.VMEM((tm, k_in), BF16)]
    else:
        tpc = CHUNK // tm
        tile_specs = [pl.BlockSpec((n_tiles, 1, dil, tm // dil, LANES),
                                   lambda i: (0, i // tpc, 0, i % tpc, 0)) for dil in dils] * 2
        scratch = [pltpu.VMEM((tm, k_in), BF16), pltpu.VMEM((N_SLABS, tm, LANES), F32)]
    ne = w_router.shape[1]
    wr = jnp.zeros((d, ROUTER_PAD), F32).at[:, :ne].set(w_router)
    wr_hi = wr.astype(BF16)
    wr_lo = (wr - wr_hi.astype(F32)).astype(BF16)
    br = jnp.zeros((1, ROUTER_PAD), F32).at[0, :ne].set(b_router)
    brow = lambda i: tk.batch_row(i * tm)
    row_spec = pl.BlockSpec((tm, d), lambda i: (i, 0))
    mod_spec = lambda j: pl.BlockSpec((1, 1, d), lambda i: (brow(i), 0, j))
    kern = functools.partial(_oproj_kernel, dils=dils, n_tiles=n_tiles)
    return pl.pallas_call(
        kern,
        grid=(n // tm,),
        in_specs=tile_specs + [
            row_spec,
            pl.BlockSpec((k_in, d), lambda i: (0, 0)),
            mod_spec(g1_j),
            pl.BlockSpec((1, d), lambda i: (0, 0)),
            mod_spec(sh_j),
            mod_spec(sc_j),
            pl.BlockSpec((d, ROUTER_PAD), lambda i: (0, 0)),
            pl.BlockSpec((d, ROUTER_PAD), lambda i: (0, 0)),
            pl.BlockSpec((1, ROUTER_PAD), lambda i: (0, 0)),
        ],
        out_specs=[row_spec, pl.BlockSpec((tm * (d // LANES), LANES), lambda i: (i, 0)),
                   pl.BlockSpec((tm, ROUTER_PAD), lambda i: (i, 0))],
        out_shape=[jax.ShapeDtypeStruct((n, d), F32), jax.ShapeDtypeStruct((n * (d // LANES), LANES), F32),
                   jax.ShapeDtypeStruct((n, ROUTER_PAD), F32)],
        scratch_shapes=scratch,
        compiler_params=_cparams(("arbitrary",)),
        name="oproj_norm_router",
    )(*os, *ls, x, w_o, mod3, ng, mod3, mod3, wr_hi, wr_lo, br)


DMA_UNROLL = 16
ISSUE_SPLIT = (7, 12, 13)


def _moe_kernel(te_ref, nu_ref, gcur_ref, gnxt_ref, sprev_ref, h_hbm, wgu_ref, bgu_ref, wdn_ref,
                bdn_ref, y_hbm, xbuf, ybuf, acc_ref, wgu_b, wdn_b, sem_g, sem_s, *, d_ff, fchunk):
    t = pl.program_id(0)
    tm, d = acc_ref.shape
    nsub = d // LANES
    half = tm // 2
    n_used = nu_ref[0]
    n_chunks = d_ff // fchunk
    assert n_chunks == 4

    def row(ref, r):
        return ref.at[pl.ds(pl.multiple_of(r * nsub, nsub), nsub)]

    def gather_row(idx_ref, r, buf):
        pltpu.make_async_copy(row(h_hbm, idx_ref[0, 0, r]), row(xbuf.at[buf], r), sem_g.at[buf]).start()

    def scatter_row(idx_ref, r, buf):
        pltpu.make_async_copy(row(ybuf.at[buf], r), row(y_hbm, idx_ref[0, 0, r]),
                              sem_s.at[buf]).start(priority=1)

    def wait_gather_half(buf):
        pltpu.make_async_copy(h_hbm.at[pl.ds(0, half * nsub)], xbuf.at[buf, pl.ds(0, half * nsub)],
                              sem_g.at[buf]).wait()

    def wait_scatter_half(buf):
        pltpu.make_async_copy(ybuf.at[buf, pl.ds(0, half * nsub)], y_hbm.at[pl.ds(0, half * nsub)],
                              sem_s.at[buf]).wait()

    def rolled(issue):
        def body(i, carry):
            for j in range(DMA_UNROLL):
                issue(i * DMA_UNROLL + j)
            return carry
        lax.fori_loop(0, tm // DMA_UNROLL, body, 0)

    @pl.when(t == 0)
    def _():
        ybuf[...] = jnp.zeros_like(ybuf)
        spare = y_hbm.shape[0] - 2 * tm * nsub
        pltpu.make_async_copy(ybuf.at[0], y_hbm.at[pl.ds(spare, tm * nsub)], sem_s.at[0]).start()
        rolled(lambda r: gather_row(gcur_ref, r, 0))
        wait_gather_half(0)

    def step(s):
        o = 1 - s

        @pl.when((t == 0) | (te_ref[t] != te_ref[jnp.maximum(t - 1, 0)]))
        def _():
            rows = 128
            def cast(i, carry):
                r0 = pl.multiple_of(i * rows, rows)
                wgu_b[pl.ds(r0, rows), :] = wgu_ref[0, 0, pl.ds(r0, rows), :].astype(BF16)
                wdn_b[pl.ds(r0, rows), :] = wdn_ref[0, 0, pl.ds(r0, rows), :].astype(BF16)
                return carry
            lax.fori_loop(0, d // rows, cast, 0)

        wait_gather_half(s)
        x = jnp.concatenate([xbuf[s, pl.ds(j, tm, stride=nsub), :] for j in range(nsub)],
                            axis=1).astype(BF16)
        issues = [("g", r) for r in range(tm)] + [("s", r) for r in range(tm)]
        n0 = ISSUE_SPLIT[0] * 2 * tm // sum(ISSUE_SPLIT)
        n1 = n0 + ISSUE_SPLIT[1] * 2 * tm // sum(ISSUE_SPLIT)
        groups = [issues[:n0], issues[n0:n1], issues[n1:], []]
        for c in range(n_chunks):
            lo, hi = c * fchunk, (c + 1) * fchunk
            if c == 1:
                wait_scatter_half(s)
            if c == 2:
                wait_scatter_half(s)
            if c == 3:
                wait_gather_half(o)
            gate = jnp.dot(x, wgu_b[:, lo:hi], preferred_element_type=F32) + bgu_ref[0, 0, :, lo:hi]
            up = jnp.dot(x, wgu_b[:, d_ff + lo:d_ff + hi],
                         preferred_element_type=F32) + bgu_ref[0, 0, :, d_ff + lo:d_ff + hi]
            gate = jnp.minimum(gate, SWIGLU_LIMIT)
            up = jnp.clip(up, -SWIGLU_LIMIT, SWIGLU_LIMIT)
            a = ((up + 1.0) * gate * jax.nn.sigmoid(SWIGLU_ALPHA * gate)).astype(BF16)
            part = jnp.dot(a, wdn_b[lo:hi, :], preferred_element_type=F32)
            if c == 0:
                acc_ref[...] = part + bdn_ref[0, 0]
            elif c < n_chunks - 1:
                acc_ref[...] += part
            else:
                for j in range(nsub):
                    cols = slice(j * LANES, (j + 1) * LANES)
                    ybuf[s, pl.ds(j, tm, stride=nsub), :] = acc_ref[:, cols] + part[:, cols]
            for kind, r in groups[c]:
                if kind == "g":
                    gather_row(gnxt_ref, r, o)
                else:
                    scatter_row(sprev_ref, r, o)

    def drain(s):
        o = 1 - s
        wait_gather_half(s)
        rolled(lambda r: scatter_row(sprev_ref, r, o))
        for buf in (0, 1):
            wait_scatter_half(buf)
            wait_scatter_half(buf)

    for s in (0, 1):
        pl.when((t < n_used) & (t % 2 == s))(functools.partial(step, s))
        pl.when((t == n_used) & (t % 2 == s))(functools.partial(drain, s))


def _moe_call(h, route, w_gu, b_gu, w_dn, b_dn, layer):
    depth, ne, d, d_ff2 = w_gu.shape
    nsub = d // LANES
    n = h.shape[0] // nsub
    d_ff = d_ff2 // 2
    tm = MOE_TILE
    n_tiles = route["tile_expert"].shape[0]
    kern = functools.partial(_moe_kernel, d_ff=d_ff, fchunk=256)
    idx_spec = lambda shift: pl.BlockSpec(
        (1, 1, tm), lambda t, te, nu: (jnp.minimum(t + shift, n_tiles - 1), 0, 0),
        memory_space=pltpu.SMEM)
    grid_spec = pltpu.PrefetchScalarGridSpec(
        num_scalar_prefetch=2,
        grid=(n_tiles,),
        in_specs=[
            idx_spec(0), idx_spec(1), idx_spec(0),
            pl.BlockSpec(memory_space=pl.ANY),
            pl.BlockSpec((1, 1, d, d_ff2), lambda t, te, nu: (layer, te[t], 0, 0)),
            pl.BlockSpec((1, 1, 1, d_ff2), lambda t, te, nu: (layer, te[t], 0, 0)),
            pl.BlockSpec((1, 1, d_ff, d), lambda t, te, nu: (layer, te[t], 0, 0)),
            pl.BlockSpec((1, 1, 1, d), lambda t, te, nu: (layer, te[t], 0, 0)),
        ],
        out_specs=pl.BlockSpec(memory_space=pl.ANY),
        scratch_shapes=[pltpu.VMEM((2, tm * nsub, LANES), F32), pltpu.VMEM((2, tm * nsub, LANES), F32),
                        pltpu.VMEM((tm, d), F32),
                        pltpu.VMEM((d, d_ff2), BF16), pltpu.VMEM((d_ff, d), BF16),
                        pltpu.SemaphoreType.DMA((2,)), pltpu.SemaphoreType.DMA((2,))],
    )
    return pl.pallas_call(
        kern,
        grid_spec=grid_spec,
        out_shape=jax.ShapeDtypeStruct(((TOP_K * n + 2 * tm) * nsub, LANES), F32),
        compiler_params=_cparams(("arbitrary",)),
        name="moe_experts",
    )(route["tile_expert"], route["n_used"], route["gsrc"], route["gsrc"], route["sdst_prev"], h,
      w_gu, b_gu.reshape(depth, ne, 1, d_ff2), w_dn, b_dn.reshape(depth, ne, 1, d))


def _route(logits, n):
    ne = N_EXPERTS
    tm = MOE_TILE
    top_val, top_idx = lax.top_k(logits[:, :ne], TOP_K)
    gates = jax.nn.softmax(top_val, axis=-1)
    e_flat = top_idx.reshape(-1).astype(jnp.int32)
    nk = n * TOP_K
    counts = jnp.sum((e_flat[:, None] == jnp.arange(ne, dtype=jnp.int32)[None, :]).astype(jnp.int32),
                     axis=0)
    padded = (counts + tm - 1) // tm * tm
    pad_ends = jnp.cumsum(padded)
    n_tiles = nk // tm + ne
    cap = n_tiles * tm
    tile_start = jnp.arange(n_tiles, dtype=jnp.int32) * tm
    tile_expert = jnp.minimum(jnp.sum((pad_ends[None, :] <= tile_start[:, None]).astype(jnp.int32), axis=1),
                              ne - 1).astype(jnp.int32)
    n_used = (pad_ends[-1] // tm).astype(jnp.int32).reshape(1)
    idx_bits = int(np.ceil(np.log2(nk + tm)))
    pad_base = (1 << idx_bits) - tm
    real_keys = (e_flat << idx_bits) + jnp.arange(nk, dtype=jnp.int32)
    lane = jnp.arange(tm, dtype=jnp.int32)[None, :]
    expert = jnp.arange(ne, dtype=jnp.int32)[:, None]
    pad_keys = jnp.where(lane < (padded - counts)[:, None], (expert << idx_bits) + pad_base + lane,
                         jnp.iinfo(jnp.int32).max)
    keys = jnp.concatenate([real_keys, pad_keys.reshape(-1)])
    low = lax.sort(keys, is_stable=False) & ((1 << idx_bits) - 1)
    valid = low < pad_base
    slot = jnp.arange(cap, dtype=jnp.int32)
    gsrc = jnp.where(valid, low // TOP_K, 0).astype(jnp.int32)
    spare = nk + ((slot // tm) % 2) * tm + slot % tm
    sdst = jnp.where(valid, (low % TOP_K) * n + low // TOP_K, spare).astype(jnp.int32)
    sdst = sdst.reshape(n_tiles, 1, tm)
    before_first = (nk + tm + jnp.arange(tm, dtype=jnp.int32)).reshape(1, 1, tm)
    sdst_prev = jnp.concatenate([before_first, sdst[:-1]], axis=0)
    return dict(gates=gates, tile_expert=tile_expert, n_used=n_used,
                gsrc=gsrc.reshape(n_tiles, 1, tm), sdst_prev=sdst_prev)


def _combine_kernel(x_ref, y0_ref, y1_ref, y2_ref, y3_ref, gt_ref, g2_ref, o_ref):
    tm, d = x_ref.shape
    nsub = d // LANES
    g = gt_ref[...]
    for j in range(nsub):
        y = None
        for k, y_ref in enumerate((y0_ref, y1_ref, y2_ref, y3_ref)):
            term = g[:, k:k + 1] * y_ref[pl.ds(j, tm, stride=nsub), :]
            y = term if y is None else y + term
        cols = slice(j * LANES, (j + 1) * LANES)
        o_ref[:, cols] = x_ref[:, cols] + g2_ref[0, :, cols] * y


def _combine_call(tk, x, ycomb, gates, mod3, g2_j):
    n, d = x.shape
    nsub = d // LANES
    tm = COMBINE_TILE
    nt = n // tm
    brow = lambda i: tk.batch_row(i * tm)
    row_spec = pl.BlockSpec((tm, d), lambda i: (i, 0))
    y_spec = lambda k: pl.BlockSpec((tm * nsub, LANES), lambda i: (k * nt + i, 0))
    return pl.pallas_call(
        _combine_kernel,
        grid=(nt,),
        in_specs=[row_spec, y_spec(0), y_spec(1), y_spec(2), y_spec(3),
                  pl.BlockSpec((tm, TOP_K), lambda i: (i, 0)),
                  pl.BlockSpec((1, 1, d), lambda i: (brow(i), 0, g2_j))],
        out_specs=row_spec,
        out_shape=jax.ShapeDtypeStruct((n, d), F32),
        compiler_params=_cparams(("arbitrary",)),
        name="moe_combine",
    )(x, ycomb, ycomb, ycomb, ycomb, gates, mod3)


def _moe_experts(tk, h, logits, w_gu, b_gu, w_dn, b_dn, layer):
    route = _route(logits, logits.shape[0])
    return _moe_call(h, route, w_gu, b_gu, w_dn, b_dn, layer), route["gates"]


@jax.jit
def _forward(x_prompt, x_sample, c_prompt, c_sample, norm_g, w_mod, b_mod, a_w_qkv, a_q_gain,
             a_k_gain, a_w_o, b_w_qkv, b_q_gain, b_k_gain, b_rpb, b_w_o, w_router, b_router,
             w_gu, b_gu, w_dn, b_dn):
    b_p, s_p, d = x_prompt.shape
    b_s, s_s, _ = x_sample.shape
    tk = _Tokens(b_p, s_p, b_s, s_s)
    depth = w_mod.shape[0]
    x = jnp.concatenate([x_prompt.reshape(-1, d), x_sample.reshape(-1, d)], axis=0)

    nb = b_p + b_s
    rows = -(-nb // 8) * 8
    c_all = jnp.zeros((rows, d), F32).at[:nb].set(jnp.concatenate([c_prompt, c_sample], axis=0))
    mods = _modulation(c_all, w_mod, b_mod)

    cos, sin = _rope_tables(tk.max_s)
    kinds_a = tuple("qkv"[cb // (len(DILATIONS) * HEADS_PER_GROUP_A // 2)]
                    for cb in range(3 * len(DILATIONS) * HEADS_PER_GROUP_A // 2))
    kinds_b = tuple("qkv"[cb // (N_HEADS_B // 2)] for cb in range(3 * N_HEADS_B // 2))

    for layer in range(depth):
        mod3 = mods[layer].reshape(rows, 1, 6 * d)
        j = layer // 2
        if layer % 2 == 0:
            qkv = _qkv_call(tk, x, norm_g[layer, 0].reshape(1, d), mod3, 0, 1,
                            a_w_qkv[j].astype(BF16), a_q_gain[j], a_k_gain[j], cos, sin, kinds_a, True,
                            dils=DILATIONS)
            groups = _dilated_attention(tk, qkv)
            os = [g[0] for g in groups]
            ls = [g[1] for g in groups]
            w_o = a_w_o[j]
            dils = DILATIONS
        else:
            dils = None
            qkv = _qkv_call(tk, x, norm_g[layer, 0].reshape(1, d), mod3, 0, 1,
                            b_w_qkv[j].astype(BF16), b_q_gain[j], b_k_gain[j], cos, sin, kinds_b, False)
            os = [_na_call(tk, qkv, _na_bias_table(b_rpb[j]))]
            ls = []
            w_o = b_w_o[j]
        x, hff, logits = _oproj_call(tk, os, ls, x, w_o.astype(BF16), mod3, 2,
                                     norm_g[layer, 1].reshape(1, d), 3, 4,
                                     w_router[layer], b_router[layer], dils=dils)
        ycomb, gates = _moe_experts(tk, hff, logits, w_gu, b_gu, w_dn, b_dn, layer)
        x = _combine_call(tk, x, ycomb, gates, mod3, 5)

    y_prompt = x[:tk.n_p].reshape(b_p, s_p, d)
    y_sample = x[tk.n_p:].reshape(b_s, s_s, d)
    return y_prompt, y_sample


def kernel(x_prompt, x_sample, c_prompt, c_sample, norm_g, w_mod, b_mod, a_w_qkv, a_q_gain, a_k_gain,
           a_w_o, b_w_qkv, b_q_gain, b_k_gain, b_rpb, b_w_o, w_router, b_router, w_gu, b_gu, w_dn, b_dn):
    return _forward(x_prompt, x_sample, c_prompt, c_sample, norm_g, w_mod, b_mod, a_w_qkv, a_q_gain,
                    a_k_gain, a_w_o, b_w_qkv, b_q_gain, b_k_gain, b_rpb, b_w_o, w_router, b_router,
                    w_gu, b_gu, w_dn, b_dn)
```

```python
import functools

import numpy as np
import jax
import jax.numpy as jnp
from jax import lax
from jax.experimental import pallas as pl
from jax.experimental.pallas import tpu as pltpu

F32 = jnp.float32
BF16 = jnp.bfloat16

HEAD_DIM = 64
ROPE_THETA = 10000.0
NORM_EPS = 1e-6
NEG_INF = -1e30
DILATIONS = (1, 4, 16)
BAND_RADIUS = 64
HEADS_PER_GROUP_A = 8
N_HEADS_B = 16
GRID_W = 64
NA_ROWS = 8
NA_COLS = 16
N_EXPERTS = 32
TOP_K = 4
SWIGLU_LIMIT = 7.0
SWIGLU_ALPHA = 1.702

LANES = 128
VMEM_LIMIT_BYTES = 56 * 1024 * 1024

CHUNK = 2048
ROW_TILE = 512
BAND_TQ = 128
NA_TQ = 512
MOE_TILE = 512
COMBINE_TILE = 256
ROUTER_PAD = LANES


def _cparams(sem):
    return pltpu.CompilerParams(dimension_semantics=sem, vmem_limit_bytes=VMEM_LIMIT_BYTES)


class _Tokens:
    def __init__(self, b_p, s_p, b_s, s_s):
        assert s_p % CHUNK == 0 and s_s % CHUNK == 0
        self.b_p, self.s_p, self.b_s, self.s_s = b_p, s_p, b_s, s_s
        self.n_p = b_p * s_p
        self.n = self.n_p + b_s * s_s
        self.n_chunks = self.n // CHUNK
        self.max_s = max(s_p, s_s)

    def batch_row(self, t0):
        return jnp.where(t0 < self.n_p, t0 // self.s_p, self.b_p + (t0 - self.n_p) // self.s_s)

    def pos(self, t0):
        return jnp.where(t0 < self.n_p, t0 % self.s_p, (t0 - self.n_p) % self.s_s)

    def seq_len(self, t0):
        return jnp.where(t0 < self.n_p, self.s_p, self.s_s)


def _mod_kernel(c_ref, w_ref, b_ref, o_ref):
    c = c_ref[...]
    act = c * jax.nn.sigmoid(c)
    o_ref[0] = jnp.dot(act, w_ref[0], precision=lax.Precision.HIGHEST,
                       preferred_element_type=F32) + b_ref[0]


def _modulation(c_all, w_mod, b_mod):
    depth, d, d6 = w_mod.shape
    rows = c_all.shape[0]
    return pl.pallas_call(
        _mod_kernel,
        grid=(depth, d6 // d),
        in_specs=[
            pl.BlockSpec((rows, d), lambda l, j: (0, 0)),
            pl.BlockSpec((1, d, d), lambda l, j: (l, 0, j)),
            pl.BlockSpec((1, 1, d), lambda l, j: (l, 0, j)),
        ],
        out_specs=pl.BlockSpec((1, rows, d), lambda l, j: (l, 0, j)),
        out_shape=jax.ShapeDtypeStruct((depth, rows, d6), F32),
        compiler_params=_cparams(("arbitrary", "arbitrary")),
        name="modulation",
    )(c_all, w_mod, b_mod.reshape(depth, 1, d6))


def _norm_modulate(x, g, shift, scale):
    y = x * lax.rsqrt(jnp.mean(x * x, axis=-1, keepdims=True) + NORM_EPS)
    return (y * g) * (1.0 + scale) + shift


N_SLABS = 4


def _qkv_kernel(x_ref, g_ref, sh_ref, sc_ref, w_ref, qg_ref, kg_ref, cos_ref, sin_ref, e_ref,
                *rest, kinds, rope, dils):
    n_out = 1 if dils is None else len(dils)
    o_refs, h_ref = rest[:n_out], rest[n_out]
    slab = None if dils is None else rest[n_out + 1]
    tm = x_ref.shape[0]
    hpg = HEADS_PER_GROUP_A // 2
    h_ref[...] = _norm_modulate(x_ref[...], g_ref[...], sh_ref[0], sc_ref[0]).astype(BF16)
    lane = lax.broadcasted_iota(jnp.int32, (1, LANES), 1)
    first_half = (lane % HEAD_DIM) < (HEAD_DIM // 2)
    n_strided = 0
    wide = None
    for c in range(len(kinds) // 2):
        if c % 2 == 0:
            wide = jnp.dot(h_ref[...], w_ref[:, c * 2 * LANES:(c + 2) * 2 * LANES],
                           preferred_element_type=F32)
        acc = wide[:, (c % 2) * 2 * LANES:(c % 2 + 1) * 2 * LANES]
        if kinds[2 * c] != "v":
            ss = jnp.dot((acc * acc).astype(BF16), e_ref[...], preferred_element_type=F32)
            acc = acc * lax.rsqrt(ss * (1.0 / HEAD_DIM) + NORM_EPS)
        for half in range(2):
            cb = 2 * c + half
            y = acc[:, half * LANES:(half + 1) * LANES]
            if kinds[cb] != "v":
                y = y * (qg_ref[...] if kinds[cb] == "q" else kg_ref[...])
                if rope:
                    partner = jnp.where(first_half, pltpu.roll(y, LANES - HEAD_DIM // 2, 1),
                                        pltpu.roll(y, HEAD_DIM // 2, 1))
                    y = y * cos_ref[...] + partner * sin_ref[...]
            if dils is None:
                o_refs[0][cb] = y.astype(BF16)
                continue
            ng = len(dils)
            kind, g, hp = cb // (ng * hpg), (cb % (ng * hpg)) // hpg, cb % hpg
            idx, dil = kind * hpg + hp, dils[g]
            if dil == 1:
                o_refs[g][idx, 0, 0] = y.astype(BF16)
            else:
                s = n_strided % N_SLABS
                n_strided += 1
                slab[s] = y
                for p in range(dil):
                    o_refs[g][idx, 0, p] = slab[s, pl.ds(p, tm // dil, stride=dil), :].astype(BF16)


def _qkv_call(tk, x, g, mod3, shift_j, scale_j, w, q_gain, k_gain, cos, sin, kinds, rope, dils=None):
    n, d = x.shape
    ncols = w.shape[1]
    ncb = ncols // LANES
    tm = ROW_TILE
    if dils is None:
        out_specs = pl.BlockSpec((ncb, tm, LANES), lambda i: (0, i, 0))
        out_shape = jax.ShapeDtypeStruct((ncb, n, LANES), BF16)
        scratch = [pltpu.VMEM((tm, d), BF16)]
    else:
        tpc = CHUNK // tm
        gcb = ncb // len(dils)
        out_specs = [pl.BlockSpec((gcb, 1, dil, tm // dil, LANES),
                                  lambda i: (0, i // tpc, 0, i % tpc, 0)) for dil in dils]
        out_shape = [jax.ShapeDtypeStruct((gcb, n // CHUNK, dil, CHUNK // dil, LANES), BF16)
                     for dil in dils]
        scratch = [pltpu.VMEM((tm, d), BF16), pltpu.VMEM((N_SLABS, tm, LANES), F32)]
    assert all(kinds[2 * c] == kinds[2 * c + 1] for c in range(ncb // 2))
    ones = np.kron(np.eye(2 * LANES // HEAD_DIM), np.ones((HEAD_DIM, HEAD_DIM))).astype(np.float32)
    e = jnp.asarray(ones, BF16)
    scale = HEAD_DIM ** -0.5
    qg = jnp.tile(q_gain.reshape(1, HEAD_DIM), (1, LANES // HEAD_DIM)) * scale
    kg = jnp.tile(k_gain.reshape(1, HEAD_DIM), (1, LANES // HEAD_DIM))
    brow = lambda i: tk.batch_row(i * tm)
    prow = lambda i: tk.pos(i * tm) // tm
    kern = functools.partial(_qkv_kernel, kinds=kinds, rope=rope, dils=dils)
    return pl.pallas_call(
        kern,
        grid=(n // tm,),
        in_specs=[
            pl.BlockSpec((tm, d), lambda i: (i, 0)),
            pl.BlockSpec((1, d), lambda i: (0, 0)),
            pl.BlockSpec((1, 1, d), lambda i: (brow(i), 0, shift_j)),
            pl.BlockSpec((1, 1, d), lambda i: (brow(i), 0, scale_j)),
            pl.BlockSpec((d, ncols), lambda i: (0, 0)),
            pl.BlockSpec((1, LANES), lambda i: (0, 0)),
            pl.BlockSpec((1, LANES), lambda i: (0, 0)),
            pl.BlockSpec((tm, LANES), lambda i: (prow(i), 0)),
            pl.BlockSpec((tm, LANES), lambda i: (prow(i), 0)),
            pl.BlockSpec((2 * LANES, 2 * LANES), lambda i: (0, 0)),
        ],
        out_specs=out_specs,
        out_shape=out_shape,
        scratch_shapes=scratch,
        compiler_params=_cparams(("arbitrary",)),
        name="norm_qkv",
    )(x, g, mod3, mod3, w, qg, kg, cos, sin, e)


def _rope_tables(max_s):
    half = HEAD_DIM // 2
    inv = ROPE_THETA ** (-np.arange(half, dtype=np.float64) / half)
    ang = np.arange(max_s, dtype=np.float64)[:, None] * inv[None, :]
    cos = np.cos(ang)
    sin = np.sin(ang)
    cos_head = np.concatenate([cos, cos], axis=1)
    sin_head = np.concatenate([-sin, sin], axis=1)
    reps = LANES // HEAD_DIM
    return (jnp.asarray(np.tile(cos_head, (1, reps)), F32),
            jnp.asarray(np.tile(sin_head, (1, reps)), F32))


def _band_kernel(q_ref, kp_ref, kc_ref, kn_ref, vp_ref, vc_ref, vn_ref, o_ref, lse_ref, *,
                 lc, chunks_p, chunks_per_seq_p):
    c = pl.program_id(0)
    i = pl.program_id(2)
    tq = BAND_TQ
    r = BAND_RADIUS
    kw = tq + 2 * r
    in_prompt = c < chunks_p
    cofs = jnp.where(in_prompt, c % chunks_per_seq_p, 0)
    seq_rows = jnp.where(in_prompt, chunks_per_seq_p, 1) * lc
    u0 = cofs * lc + i * tq
    row = lax.broadcasted_iota(jnp.int32, (tq, kw), 0)
    col = lax.broadcasted_iota(jnp.int32, (tq, kw), 1)
    uk = u0 - r + col
    valid = (jnp.abs(col - r - row) <= r) & (uk >= 0) & (uk < seq_rows)
    lane = lax.broadcasted_iota(jnp.int32, (1, LANES), 1)
    lo = lane < HEAD_DIM
    hi = jnp.logical_not(lo)
    nhp = q_ref.shape[0]
    qs, ks, vs = [], [], []
    for hp in range(nhp):
        q = q_ref[hp, 0, 0]
        k = jnp.concatenate([kp_ref[hp, 0, 0], kc_ref[hp, 0, 0], kn_ref[hp, 0, 0]], axis=0)
        v = jnp.concatenate([vp_ref[hp, 0, 0], vc_ref[hp, 0, 0], vn_ref[hp, 0, 0]], axis=0)
        qs += [jnp.where(lo, q, jnp.zeros_like(q)), jnp.where(hi, q, jnp.zeros_like(q))]
        ks += [k, k]
        vs += [v, v]
    s = lax.dot_general(jnp.stack(qs), jnp.stack(ks), (((2,), (2,)), ((0,), (0,))),
                        preferred_element_type=F32)
    s = jnp.where(valid[None], s, NEG_INF)
    m = jnp.max(s, axis=-1, keepdims=True)
    p = jnp.exp(s - m)
    den = jnp.sum(p, axis=-1, keepdims=True)
    o = lax.dot_general(p.astype(BF16), jnp.stack(vs), (((2,), (1,)), ((0,), (0,))),
                        preferred_element_type=F32) / den
    lse = m + jnp.log(den)
    for hp in range(nhp):
        o_ref[hp, 0, 0] = jnp.where(lo, o[2 * hp], o[2 * hp + 1]).astype(BF16)
        lse_ref[hp, 0, 0] = jnp.where(lo, lse[2 * hp], lse[2 * hp + 1])


def _band_call(tk, arr, q_blk, k_blk, v_blk, dil):
    nch = tk.n_chunks
    lc = CHUNK // dil
    tq = BAND_TQ
    r = BAND_RADIUS
    hpg = HEADS_PER_GROUP_A // 2
    per = tq // r
    nb = lc // r

    def cur(blk):
        return pl.BlockSpec((hpg, 1, 1, tq, LANES), lambda c, p, i: (blk, c, p, i, 0))

    def prev(blk):
        def imap(c, p, i):
            j = per * i - 1
            return (blk, jnp.where(j < 0, jnp.maximum(c - 1, 0), c), p, jnp.where(j < 0, nb - 1, j), 0)
        return pl.BlockSpec((hpg, 1, 1, r, LANES), imap)

    def nxt(blk):
        def imap(c, p, i):
            j = per * (i + 1)
            return (blk, jnp.where(j >= nb, jnp.minimum(c + 1, nch - 1), c), p,
                    jnp.where(j >= nb, 0, j), 0)
        return pl.BlockSpec((hpg, 1, 1, r, LANES), imap)

    kern = functools.partial(_band_kernel, lc=lc, chunks_p=tk.n_p // CHUNK,
                             chunks_per_seq_p=tk.s_p // CHUNK)
    out_spec = pl.BlockSpec((hpg, 1, 1, tq, LANES), lambda c, p, i: (0, c, p, i, 0))
    return pl.pallas_call(
        kern,
        grid=(nch, dil, lc // tq),
        in_specs=[cur(q_blk), prev(k_blk), cur(k_blk), nxt(k_blk), prev(v_blk), cur(v_blk), nxt(v_blk)],
        out_specs=[out_spec, out_spec],
        out_shape=[jax.ShapeDtypeStruct((hpg, nch, dil, lc, LANES), BF16),
                   jax.ShapeDtypeStruct((hpg, nch, dil, lc, LANES), F32)],
        compiler_params=_cparams(("arbitrary", "arbitrary", "arbitrary")),
        name="band_attention",
    )(arr, arr, arr, arr, arr, arr, arr)


def _dilated_attention(tk, qkv_groups):
    return [_band_call(tk, arr, 0, 1, 2, dil) for arr, dil in zip(qkv_groups, DILATIONS)]


def _na_kernel(q_ref, kp_ref, kc_ref, kn_ref, vp_ref, vc_ref, vn_ref, bias_ref, o_ref, kbuf, vbuf, *,
               rows_p, rows_s, blocks_p):
    bq = pl.program_id(1)
    tq = NA_TQ
    rows_per_blk = tq // GRID_W
    in_prompt = bq < blocks_p
    rows = jnp.where(in_prompt, rows_p, rows_s)
    blk_in_seq = jnp.where(in_prompt, bq % (rows_p // rows_per_blk),
                           (bq - blocks_p) % (rows_s // rows_per_blk))
    r0 = blk_in_seq * rows_per_blk
    kbuf[0 * tq:1 * tq] = kp_ref[0]
    kbuf[1 * tq:2 * tq] = kc_ref[0]
    kbuf[2 * tq:3 * tq] = kn_ref[0]
    vbuf[0 * tq:1 * tq] = vp_ref[0]
    vbuf[1 * tq:2 * tq] = vc_ref[0]
    vbuf[2 * tq:3 * tq] = vn_ref[0]
    lane = lax.broadcasted_iota(jnp.int32, (1, LANES), 1)
    lo = lane < HEAD_DIM
    hi = jnp.logical_not(lo)
    nkeys = NA_ROWS * GRID_W
    qs, ks, vs, bs = [], [], [], []
    for d in range(rows_per_blk):
        r = r0 + d
        rs = jnp.clip(r - NA_ROWS // 2, 0, rows - NA_ROWS)
        delta = r - rs
        off = pl.multiple_of((rs - (r0 - rows_per_blk)) * GRID_W, GRID_W)
        k = kbuf[pl.ds(off, nkeys), :]
        v = vbuf[pl.ds(off, nkeys), :]
        q = q_ref[0, d * GRID_W:(d + 1) * GRID_W, :]
        qs += [jnp.where(lo, q, jnp.zeros_like(q)), jnp.where(hi, q, jnp.zeros_like(q))]
        ks += [k, k]
        vs += [v, v]
        bs += [bias_ref[0, 0, delta], bias_ref[0, 1, delta]]
    s = lax.dot_general(jnp.stack(qs), jnp.stack(ks), (((2,), (2,)), ((0,), (0,))),
                        preferred_element_type=F32)
    s = s + jnp.stack(bs)
    m = jnp.max(s, axis=-1, keepdims=True)
    p = jnp.exp(s - m)
    den = jnp.sum(p, axis=-1, keepdims=True)
    o = lax.dot_general(p.astype(BF16), jnp.stack(vs), (((2,), (1,)), ((0,), (0,))),
                        preferred_element_type=F32) / den
    for d in range(rows_per_blk):
        o_ref[0, d * GRID_W:(d + 1) * GRID_W, :] = jnp.where(lo, o[2 * d], o[2 * d + 1]).astype(BF16)


def _na_bias_table(rpb):
    qc = np.arange(GRID_W)
    cstart = np.clip(qc - NA_COLS // 2, 0, GRID_W - NA_COLS)
    kc = np.arange(GRID_W)
    col_valid = (kc[None, :] >= cstart[:, None]) & (kc[None, :] < cstart[:, None] + NA_COLS)
    col_off = np.clip(kc[None, :] - qc[:, None] + NA_COLS - 1, 0, 2 * NA_COLS - 2)
    delta = np.arange(NA_ROWS)
    row_off = np.arange(NA_ROWS)[None, :] - delta[:, None] + NA_ROWS - 1
    t = rpb[:, row_off]
    t = t[:, :, :, col_off]
    t = jnp.where(jnp.asarray(col_valid)[None, None, None], t, NEG_INF)
    t = jnp.transpose(t, (0, 1, 3, 2, 4))
    h = rpb.shape[0]
    return t.reshape(h // 2, 2, NA_ROWS, GRID_W, NA_ROWS * GRID_W).astype(F32)


def _na_call(tk, qkv, bias):
    n = tk.n
    tq = NA_TQ
    nhp = N_HEADS_B // 2
    nblk = n // tq
    arr = qkv.reshape(3 * nhp, nblk, tq, LANES)
    nkeys = NA_ROWS * GRID_W

    arr3 = arr
    kern = functools.partial(_na_kernel, rows_p=tk.s_p // GRID_W, rows_s=tk.s_s // GRID_W,
                             blocks_p=tk.n_p // tq)

    def bspec(base, shift):
        return pl.BlockSpec((None, 1, tq, LANES),
                            lambda hp, b: (base + hp, jnp.clip(b + shift, 0, nblk - 1), 0, 0))

    return pl.pallas_call(
        kern,
        grid=(nhp, nblk),
        in_specs=[bspec(0, 0), bspec(nhp, -1), bspec(nhp, 0), bspec(nhp, 1),
                  bspec(2 * nhp, -1), bspec(2 * nhp, 0), bspec(2 * nhp, 1),
                  pl.BlockSpec((1, 2, NA_ROWS, GRID_W, nkeys), lambda hp, b: (hp, 0, 0, 0, 0))],
        out_specs=pl.BlockSpec((None, 1, tq, LANES), lambda hp, b: (hp, b, 0, 0)),
        out_shape=jax.ShapeDtypeStruct((nhp, nblk, tq, LANES), BF16),
        scratch_shapes=[pltpu.VMEM((3 * tq, LANES), BF16), pltpu.VMEM((3 * tq, LANES), BF16)],
        compiler_params=_cparams(("arbitrary", "arbitrary")),
        name="neighbourhood_attention",
    )(arr3, arr3, arr3, arr3, arr3, arr3, arr3, bias).reshape(nhp, n, LANES)


def _oproj_kernel(*refs, dils, n_tiles):
    n_groups = 1 if dils is None else len(dils)
    o_refs = refs[:n_groups]
    l_refs = refs[n_groups:2 * n_groups] if dils is not None else ()
    k = len(o_refs) + len(l_refs)
    (x_ref, wo_ref, g1_ref, ng_ref, sh_ref, sc_ref, wrh_ref, wrl_ref, br_ref,
     xo_ref, h_ref, lg_ref, m_ref) = refs[k:k + 13]
    slab = refs[k + 13] if dils is not None else None
    tm = x_ref.shape[0]
    n_strided = [0]

    def natural(ref, t, dil):
        if dil == 1:
            return ref[t, 0, 0].astype(F32)
        s = n_strided[0] % N_SLABS
        n_strided[0] += 1
        for p in range(dil):
            slab[s, pl.ds(p, tm // dil, stride=dil), :] = ref[t, 0, p].astype(F32)
        return slab[s]

    for t in range(n_tiles):
        if dils is None:
            m_ref[:, t * LANES:(t + 1) * LANES] = o_refs[0][t]
        else:
            ls = [natural(l, t, dil) for l, dil in zip(l_refs, dils)]
            mx = functools.reduce(jnp.maximum, ls)
            es = [jnp.exp(l - mx) for l in ls]
            den = functools.reduce(lambda a, b: a + b, es)
            num = functools.reduce(lambda a, b: a + b,
                                   [e * natural(o, t, dil) for e, o, dil in zip(es, o_refs, dils)])
            m_ref[:, t * LANES:(t + 1) * LANES] = (num / den).astype(BF16)
    mix = jnp.dot(m_ref[...], wo_ref[...], preferred_element_type=F32)
    x = x_ref[...] + g1_ref[0] * mix
    xo_ref[...] = x
    h = _norm_modulate(x, ng_ref[...], sh_ref[0], sc_ref[0])
    hi = h.astype(BF16)
    lo = (h - hi.astype(F32)).astype(BF16)
    tm, d = h.shape
    nsub = d // LANES
    for j in range(nsub):
        h_ref[pl.ds(j, tm, stride=nsub), :] = h[:, j * LANES:(j + 1) * LANES]
    lg = jnp.dot(hi, wrh_ref[...], preferred_element_type=F32)
    lg = lg + jnp.dot(lo, wrh_ref[...], preferred_element_type=F32)
    lg = lg + jnp.dot(hi, wrl_ref[...], preferred_element_type=F32)
    lg_ref[...] = lg + br_ref[...]


def _oproj_call(tk, os, ls, x, w_o, mod3, g1_j, ng, sh_j, sc_j, w_router, b_router, dils=None):
    n, d = x.shape
    tm = ROW_TILE
    n_tiles = os[0].shape[0]
    k_in = n_tiles * LANES
    if dils is None:
        tile_specs = [pl.BlockSpec((n_tiles, tm, LANES), lambda i: (0, i, 0))]
        scratch = [pltpu.VMEM((tm, k_in), BF16)]
    else:
        tpc = CHUNK // tm
        tile_specs = [pl.BlockSpec((n_tiles, 1, dil, tm // dil, LANES),
                                   lambda i: (0, i // tpc, 0, i % tpc, 0)) for dil in dils] * 2
        scratch = [pltpu.VMEM((tm, k_in), BF16), pltpu.VMEM((N_SLABS, tm, LANES), F32)]
    ne = w_router.shape[1]
    wr = jnp.zeros((d, ROUTER_PAD), F32).at[:, :ne].set(w_router)
    wr_hi = wr.astype(BF16)
    wr_lo = (wr - wr_hi.astype(F32)).astype(BF16)
    br = jnp.zeros((1, ROUTER_PAD), F32).at[0, :ne].set(b_router)
    brow = lambda i: tk.batch_row(i * tm)
    row_spec = pl.BlockSpec((tm, d), lambda i: (i, 0))
    mod_spec = lambda j: pl.BlockSpec((1, 1, d), lambda i: (brow(i), 0, j))
    kern = functools.partial(_oproj_kernel, dils=dils, n_tiles=n_tiles)
    return pl.pallas_call(
        kern,
        grid=(n // tm,),
        in_specs=tile_specs + [
            row_spec,
            pl.BlockSpec((k_in, d), lambda i: (0, 0)),
            mod_spec(g1_j),
            pl.BlockSpec((1, d), lambda i: (0, 0)),
            mod_spec(sh_j),
            mod_spec(sc_j),
            pl.BlockSpec((d, ROUTER_PAD), lambda i: (0, 0)),
            pl.BlockSpec((d, ROUTER_PAD), lambda i: (0, 0)),
            pl.BlockSpec((1, ROUTER_PAD), lambda i: (0, 0)),
        ],
        out_specs=[row_spec, pl.BlockSpec((tm * (d // LANES), LANES), lambda i: (i, 0)),
                   pl.BlockSpec((tm, ROUTER_PAD), lambda i: (i, 0))],
        out_shape=[jax.ShapeDtypeStruct((n, d), F32), jax.ShapeDtypeStruct((n * (d // LANES), LANES), F32),
                   jax.ShapeDtypeStruct((n, ROUTER_PAD), F32)],
        scratch_shapes=scratch,
        compiler_params=_cparams(("arbitrary",)),
        name="oproj_norm_router",
    )(*os, *ls, x, w_o, mod3, ng, mod3, mod3, wr_hi, wr_lo, br)


DMA_UNROLL = 16
ISSUE_SPLIT = (7, 12, 13)


def _moe_kernel(te_ref, nu_ref, gcur_ref, gnxt_ref, sprev_ref, h_hbm, wgu_ref, bgu_ref, wdn_ref,
                bdn_ref, y_hbm, xbuf, ybuf, acc_ref, wgu_b, wdn_b, sem_g, sem_s, *, d_ff, fchunk):
    t = pl.program_id(0)
    tm, d = acc_ref.shape
    nsub = d // LANES
    half = tm // 2
    n_used = nu_ref[0]
    n_chunks = d_ff // fchunk
    assert n_chunks == 4

    def row(ref, r):
        return ref.at[pl.ds(pl.multiple_of(r * nsub, nsub), nsub)]

    def gather_row(idx_ref, r, buf):
        pltpu.make_async_copy(row(h_hbm, idx_ref[0, 0, r]), row(xbuf.at[buf], r), sem_g.at[buf]).start()

    def scatter_row(idx_ref, r, buf):
        pltpu.make_async_copy(row(ybuf.at[buf], r), row(y_hbm, idx_ref[0, 0, r]),
                              sem_s.at[buf]).start(priority=1)

    def wait_gather_half(buf):
        pltpu.make_async_copy(h_hbm.at[pl.ds(0, half * nsub)], xbuf.at[buf, pl.ds(0, half * nsub)],
                              sem_g.at[buf]).wait()

    def wait_scatter_half(buf):
        pltpu.make_async_copy(ybuf.at[buf, pl.ds(0, half * nsub)], y_hbm.at[pl.ds(0, half * nsub)],
                              sem_s.at[buf]).wait()

    def rolled(issue):
        def body(i, carry):
            for j in range(DMA_UNROLL):
                issue(i * DMA_UNROLL + j)
            return carry
        lax.fori_loop(0, tm // DMA_UNROLL, body, 0)

    @pl.when(t == 0)
    def _():
        ybuf[...] = jnp.zeros_like(ybuf)
        spare = y_hbm.shape[0] - 2 * tm * nsub
        pltpu.make_async_copy(ybuf.at[0], y_hbm.at[pl.ds(spare, tm * nsub)], sem_s.at[0]).start()
        rolled(lambda r: gather_row(gcur_ref, r, 0))
        wait_gather_half(0)

    def step(s):
        o = 1 - s

        @pl.when((t == 0) | (te_ref[t] != te_ref[jnp.maximum(t - 1, 0)]))
        def _():
            rows = 128
            def cast(i, carry):
                r0 = pl.multiple_of(i * rows, rows)
                wgu_b[pl.ds(r0, rows), :] = wgu_ref[0, 0, pl.ds(r0, rows), :].astype(BF16)
                wdn_b[pl.ds(r0, rows), :] = wdn_ref[0, 0, pl.ds(r0, rows), :].astype(BF16)
                return carry
            lax.fori_loop(0, d // rows, cast, 0)

        wait_gather_half(s)
        x = jnp.concatenate([xbuf[s, pl.ds(j, tm, stride=nsub), :] for j in range(nsub)],
                            axis=1).astype(BF16)
        issues = [("g", r) for r in range(tm)] + [("s", r) for r in range(tm)]
        n0 = ISSUE_SPLIT[0] * 2 * tm // sum(ISSUE_SPLIT)
        n1 = n0 + ISSUE_SPLIT[1] * 2 * tm // sum(ISSUE_SPLIT)
        groups = [issues[:n0], issues[n0:n1], issues[n1:], []]
        for c in range(n_chunks):
            lo, hi = c * fchunk, (c + 1) * fchunk
            if c == 1:
                wait_scatter_half(s)
            if c == 2:
                wait_scatter_half(s)
            if c == 3:
                wait_gather_half(o)
            gate = jnp.dot(x, wgu_b[:, lo:hi], preferred_element_type=F32) + bgu_ref[0, 0, :, lo:hi]
            up = jnp.dot(x, wgu_b[:, d_ff + lo:d_ff + hi],
                         preferred_element_type=F32) + bgu_ref[0, 0, :, d_ff + lo:d_ff + hi]
            gate = jnp.minimum(gate, SWIGLU_LIMIT)
            up = jnp.clip(up, -SWIGLU_LIMIT, SWIGLU_LIMIT)
            a = ((up + 1.0) * gate * jax.nn.sigmoid(SWIGLU_ALPHA * gate)).astype(BF16)
            part = jnp.dot(a, wdn_b[lo:hi, :], preferred_element_type=F32)
            if c == 0:
                acc_ref[...] = part + bdn_ref[0, 0]
            elif c < n_chunks - 1:
                acc_ref[...] += part
            else:
                for j in range(nsub):
                    cols = slice(j * LANES, (j + 1) * LANES)
                    ybuf[s, pl.ds(j, tm, stride=nsub), :] = acc_ref[:, cols] + part[:, cols]
            for kind, r in groups[c]:
                if kind == "g":
                    gather_row(gnxt_ref, r, o)
                else:
                    scatter_row(sprev_ref, r, o)

    def drain(s):
        o = 1 - s
        wait_gather_half(s)
        rolled(lambda r: scatter_row(sprev_ref, r, o))
        for buf in (0, 1):
            wait_scatter_half(buf)
            wait_scatter_half(buf)

    pl.when(t < n_used)(functools.partial(step, t % 2))
    pl.when(t == n_used)(functools.partial(drain, t % 2))


def _moe_call(h, route, w_gu, b_gu, w_dn, b_dn, layer):
    depth, ne, d, d_ff2 = w_gu.shape
    nsub = d // LANES
    n = h.shape[0] // nsub
    d_ff = d_ff2 // 2
    tm = MOE_TILE
    n_tiles = route["tile_expert"].shape[0]
    kern = functools.partial(_moe_kernel, d_ff=d_ff, fchunk=256)
    idx_spec = lambda shift: pl.BlockSpec(
        (1, 1, tm), lambda t, te, nu: (jnp.minimum(t + shift, n_tiles - 1), 0, 0),
        memory_space=pltpu.SMEM)
    grid_spec = pltpu.PrefetchScalarGridSpec(
        num_scalar_prefetch=2,
        grid=(n_tiles,),
        in_specs=[
            idx_spec(0), idx_spec(1), idx_spec(0),
            pl.BlockSpec(memory_space=pl.ANY),
            pl.BlockSpec((1, 1, d, d_ff2), lambda t, te, nu: (layer, te[t], 0, 0)),
            pl.BlockSpec((1, 1, 1, d_ff2), lambda t, te, nu: (layer, te[t], 0, 0)),
            pl.BlockSpec((1, 1, d_ff, d), lambda t, te, nu: (layer, te[t], 0, 0)),
            pl.BlockSpec((1, 1, 1, d), lambda t, te, nu: (layer, te[t], 0, 0)),
        ],
        out_specs=pl.BlockSpec(memory_space=pl.ANY),
        scratch_shapes=[pltpu.VMEM((2, tm * nsub, LANES), F32), pltpu.VMEM((2, tm * nsub, LANES), F32),
                        pltpu.VMEM((tm, d), F32),
                        pltpu.VMEM((d, d_ff2), BF16), pltpu.VMEM((d_ff, d), BF16),
                        pltpu.SemaphoreType.DMA((2,)), pltpu.SemaphoreType.DMA((2,))],
    )
    return pl.pallas_call(
        kern,
        grid_spec=grid_spec,
        out_shape=jax.ShapeDtypeStruct(((TOP_K * n + 2 * tm) * nsub, LANES), F32),
        compiler_params=_cparams(("arbitrary",)),
        name="moe_experts",
    )(route["tile_expert"], route["n_used"], route["gsrc"], route["gsrc"], route["sdst_prev"], h,
      w_gu, b_gu.reshape(depth, ne, 1, d_ff2), w_dn, b_dn.reshape(depth, ne, 1, d))


def _route(logits, n):
    ne = N_EXPERTS
    tm = MOE_TILE
    top_val, top_idx = lax.top_k(logits[:, :ne], TOP_K)
    gates = jax.nn.softmax(top_val, axis=-1)
    e_flat = top_idx.reshape(-1).astype(jnp.int32)
    nk = n * TOP_K
    counts = jnp.sum((e_flat[:, None] == jnp.arange(ne, dtype=jnp.int32)[None, :]).astype(jnp.int32),
                     axis=0)
    padded = (counts + tm - 1) // tm * tm
    pad_ends = jnp.cumsum(padded)
    n_tiles = nk // tm + ne
    cap = n_tiles * tm
    tile_start = jnp.arange(n_tiles, dtype=jnp.int32) * tm
    tile_expert = jnp.minimum(jnp.sum((pad_ends[None, :] <= tile_start[:, None]).astype(jnp.int32), axis=1),
                              ne - 1).astype(jnp.int32)
    n_used = (pad_ends[-1] // tm).astype(jnp.int32).reshape(1)
    idx_bits = int(np.ceil(np.log2(nk + tm)))
    pad_base = (1 << idx_bits) - tm
    real_keys = (e_flat << idx_bits) + jnp.arange(nk, dtype=jnp.int32)
    lane = jnp.arange(tm, dtype=jnp.int32)[None, :]
    expert = jnp.arange(ne, dtype=jnp.int32)[:, None]
    pad_keys = jnp.where(lane < (padded - counts)[:, None], (expert << idx_bits) + pad_base + lane,
                         jnp.iinfo(jnp.int32).max)
    keys = jnp.concatenate([real_keys, pad_keys.reshape(-1)])
    low = lax.sort(keys, is_stable=False) & ((1 << idx_bits) - 1)
    valid = low < pad_base
    slot = jnp.arange(cap, dtype=jnp.int32)
    gsrc = jnp.where(valid, low // TOP_K, 0).astype(jnp.int32)
    spare = nk + ((slot // tm) % 2) * tm + slot % tm
    sdst = jnp.where(valid, (low % TOP_K) * n + low // TOP_K, spare).astype(jnp.int32)
    sdst = sdst.reshape(n_tiles, 1, tm)
    before_first = (nk + tm + jnp.arange(tm, dtype=jnp.int32)).reshape(1, 1, tm)
    sdst_prev = jnp.concatenate([before_first, sdst[:-1]], axis=0)
    return dict(gates=gates, tile_expert=tile_expert, n_used=n_used,
                gsrc=gsrc.reshape(n_tiles, 1, tm), sdst_prev=sdst_prev)


def _combine_kernel(x_ref, y0_ref, y1_ref, y2_ref, y3_ref, gt_ref, g2_ref, o_ref):
    tm, d = x_ref.shape
    nsub = d // LANES
    g = gt_ref[...]
    for j in range(nsub):
        y = None
        for k, y_ref in enumerate((y0_ref, y1_ref, y2_ref, y3_ref)):
            term = g[:, k:k + 1] * y_ref[pl.ds(j, tm, stride=nsub), :]
            y = term if y is None else y + term
        cols = slice(j * LANES, (j + 1) * LANES)
        o_ref[:, cols] = x_ref[:, cols] + g2_ref[0, :, cols] * y


def _combine_call(tk, x, ycomb, gates, mod3, g2_j):
    n, d = x.shape
    nsub = d // LANES
    tm = COMBINE_TILE
    nt = n // tm
    brow = lambda i: tk.batch_row(i * tm)
    row_spec = pl.BlockSpec((tm, d), lambda i: (i, 0))
    y_spec = lambda k: pl.BlockSpec((tm * nsub, LANES), lambda i: (k * nt + i, 0))
    return pl.pallas_call(
        _combine_kernel,
        grid=(nt,),
        in_specs=[row_spec, y_spec(0), y_spec(1), y_spec(2), y_spec(3),
                  pl.BlockSpec((tm, TOP_K), lambda i: (i, 0)),
                  pl.BlockSpec((1, 1, d), lambda i: (brow(i), 0, g2_j))],
        out_specs=row_spec,
        out_shape=jax.ShapeDtypeStruct((n, d), F32),
        compiler_params=_cparams(("arbitrary",)),
        name="moe_combine",
    )(x, ycomb, ycomb, ycomb, ycomb, gates, mod3)


def _moe_experts(tk, h, logits, w_gu, b_gu, w_dn, b_dn, layer):
    route = _route(logits, logits.shape[0])
    return _moe_call(h, route, w_gu, b_gu, w_dn, b_dn, layer), route["gates"]


@jax.jit
def _forward(x_prompt, x_sample, c_prompt, c_sample, norm_g, w_mod, b_mod, a_w_qkv, a_q_gain,
             a_k_gain, a_w_o, b_w_qkv, b_q_gain, b_k_gain, b_rpb, b_w_o, w_router, b_router,
             w_gu, b_gu, w_dn, b_dn):
    b_p, s_p, d = x_prompt.shape
    b_s, s_s, _ = x_sample.shape
    tk = _Tokens(b_p, s_p, b_s, s_s)
    depth = w_mod.shape[0]
    x = jnp.concatenate([x_prompt.reshape(-1, d), x_sample.reshape(-1, d)], axis=0)

    nb = b_p + b_s
    rows = -(-nb // 8) * 8
    c_all = jnp.zeros((rows, d), F32).at[:nb].set(jnp.concatenate([c_prompt, c_sample], axis=0))
    mods = _modulation(c_all, w_mod, b_mod)

    cos, sin = _rope_tables(tk.max_s)
    kinds_a = tuple("qkv"[cb // (len(DILATIONS) * HEADS_PER_GROUP_A // 2)]
                    for cb in range(3 * len(DILATIONS) * HEADS_PER_GROUP_A // 2))
    kinds_b = tuple("qkv"[cb // (N_HEADS_B // 2)] for cb in range(3 * N_HEADS_B // 2))

    for layer in range(depth):
        mod3 = mods[layer].reshape(rows, 1, 6 * d)
        j = layer // 2
        if layer % 2 == 0:
            qkv = _qkv_call(tk, x, norm_g[layer, 0].reshape(1, d), mod3, 0, 1,
                            a_w_qkv[j].astype(BF16), a_q_gain[j], a_k_gain[j], cos, sin, kinds_a, True,
                            dils=DILATIONS)
            groups = _dilated_attention(tk, qkv)
            os = [g[0] for g in groups]
            ls = [g[1] for g in groups]
            w_o = a_w_o[j]
            dils = DILATIONS
        else:
            dils = None
            qkv = _qkv_call(tk, x, norm_g[layer, 0].reshape(1, d), mod3, 0, 1,
                            b_w_qkv[j].astype(BF16), b_q_gain[j], b_k_gain[j], cos, sin, kinds_b, False)
            os = [_na_call(tk, qkv, _na_bias_table(b_rpb[j]))]
            ls = []
            w_o = b_w_o[j]
        x, hff, logits = _oproj_call(tk, os, ls, x, w_o.astype(BF16), mod3, 2,
                                     norm_g[layer, 1].reshape(1, d), 3, 4,
                                     w_router[layer], b_router[layer], dils=dils)
        ycomb, gates = _moe_experts(tk, hff, logits, w_gu, b_gu, w_dn, b_dn, layer)
        x = _combine_call(tk, x, ycomb, gates, mod3, 5)

    y_prompt = x[:tk.n_p].reshape(b_p, s_p, d)
    y_sample = x[tk.n_p:].reshape(b_s, s_s, d)
    return y_prompt, y_sample


def kernel(x_prompt, x_sample, c_prompt, c_sample, norm_g, w_mod, b_mod, a_w_qkv, a_q_gain, a_k_gain,
           a_w_o, b_w_qkv, b_q_gain, b_k_gain, b_rpb, b_w_o, w_router, b_router, w_gu, b_gu, w_dn, b_dn):
    return _forward(x_prompt, x_sample, c_prompt, c_sample, norm_g, w_mod, b_mod, a_w_qkv, a_q_gain,
                    a_k_gain, a_w_o, b_w_qkv, b_q_gain, b_k_gain, b_rpb, b_w_o, w_router, b_router,
                    w_gu, b_gu, w_dn, b_dn)
```

```python
import functools

import numpy as np
import jax
import jax.numpy as jnp
from jax import lax
from jax.experimental import pallas as pl
from jax.experimental.pallas import tpu as pltpu

F32 = jnp.float32
BF16 = jnp.bfloat16

HEAD_DIM = 64
ROPE_THETA = 10000.0
NORM_EPS = 1e-6
NEG_INF = -1e30
DILATIONS = (1, 4, 16)
BAND_RADIUS = 64
HEADS_PER_GROUP_A = 8
N_HEADS_B = 16
GRID_W = 64
NA_ROWS = 8
NA_COLS = 16
N_EXPERTS = 32
TOP_K = 4
SWIGLU_LIMIT = 7.0
SWIGLU_ALPHA = 1.702

LANES = 128
VMEM_LIMIT_BYTES = 56 * 1024 * 1024

CHUNK = 2048
ROW_TILE = 512
BAND_TQ = 128
NA_TQ = 512
MOE_TILE = 512
COMBINE_TILE = 256
ROUTER_PAD = LANES


def _cparams(sem):
    return pltpu.CompilerParams(dimension_semantics=sem, vmem_limit_bytes=VMEM_LIMIT_BYTES)


class _Tokens:
    def __init__(self, b_p, s_p, b_s, s_s):
        assert s_p % CHUNK == 0 and s_s % CHUNK == 0
        self.b_p, self.s_p, self.b_s, self.s_s = b_p, s_p, b_s, s_s
        self.n_p = b_p * s_p
        self.n = self.n_p + b_s * s_s
        self.n_chunks = self.n // CHUNK
        self.max_s = max(s_p, s_s)

    def batch_row(self, t0):
        return jnp.where(t0 < self.n_p, t0 // self.s_p, self.b_p + (t0 - self.n_p) // self.s_s)

    def pos(self, t0):
        return jnp.where(t0 < self.n_p, t0 % self.s_p, (t0 - self.n_p) % self.s_s)

    def seq_len(self, t0):
        return jnp.where(t0 < self.n_p, self.s_p, self.s_s)


def _mod_kernel(c_ref, w_ref, b_ref, o_ref):
    c = c_ref[...]
    act = c * jax.nn.sigmoid(c)
    o_ref[0] = jnp.dot(act, w_ref[0], precision=lax.Precision.HIGHEST,
                       preferred_element_type=F32) + b_ref[0]


def _modulation(c_all, w_mod, b_mod):
    depth, d, d6 = w_mod.shape
    rows = c_all.shape[0]
    return pl.pallas_call(
        _mod_kernel,
        grid=(depth, d6 // d),
        in_specs=[
            pl.BlockSpec((rows, d), lambda l, j: (0, 0)),
            pl.BlockSpec((1, d, d), lambda l, j: (l, 0, j)),
            pl.BlockSpec((1, 1, d), lambda l, j: (l, 0, j)),
        ],
        out_specs=pl.BlockSpec((1, rows, d), lambda l, j: (l, 0, j)),
        out_shape=jax.ShapeDtypeStruct((depth, rows, d6), F32),
        compiler_params=_cparams(("arbitrary", "arbitrary")),
        name="modulation",
    )(c_all, w_mod, b_mod.reshape(depth, 1, d6))


def _norm_modulate(x, g, shift, scale):
    y = x * lax.rsqrt(jnp.mean(x * x, axis=-1, keepdims=True) + NORM_EPS)
    return (y * g) * (1.0 + scale) + shift


N_SLABS = 4


def _qkv_kernel(x_ref, g_ref, sh_ref, sc_ref, w_ref, qg_ref, kg_ref, cos_ref, sin_ref, e_ref,
                *rest, kinds, rope, dils):
    n_out = 1 if dils is None else len(dils)
    o_refs, h_ref = rest[:n_out], rest[n_out]
    slab = None if dils is None else rest[n_out + 1]
    tm = x_ref.shape[0]
    hpg = HEADS_PER_GROUP_A // 2
    h_ref[...] = _norm_modulate(x_ref[...], g_ref[...], sh_ref[0], sc_ref[0]).astype(BF16)
    lane = lax.broadcasted_iota(jnp.int32, (1, LANES), 1)
    first_half = (lane % HEAD_DIM) < (HEAD_DIM // 2)
    n_strided = 0
    wide = None
    for c in range(len(kinds) // 2):
        if c % 2 == 0:
            wide = jnp.dot(h_ref[...], w_ref[:, c * 2 * LANES:(c + 2) * 2 * LANES],
                           preferred_element_type=F32)
        acc = wide[:, (c % 2) * 2 * LANES:(c % 2 + 1) * 2 * LANES]
        if kinds[2 * c] != "v":
            ss = jnp.dot((acc * acc).astype(BF16), e_ref[...], preferred_element_type=F32)
            acc = acc * lax.rsqrt(ss * (1.0 / HEAD_DIM) + NORM_EPS)
        for half in range(2):
            cb = 2 * c + half
            y = acc[:, half * LANES:(half + 1) * LANES]
            if kinds[cb] != "v":
                y = y * (qg_ref[...] if kinds[cb] == "q" else kg_ref[...])
                if rope:
                    partner = jnp.where(first_half, pltpu.roll(y, LANES - HEAD_DIM // 2, 1),
                                        pltpu.roll(y, HEAD_DIM // 2, 1))
                    y = y * cos_ref[...] + partner * sin_ref[...]
            if dils is None:
                o_refs[0][cb] = y.astype(BF16)
                continue
            ng = len(dils)
            kind, g, hp = cb // (ng * hpg), (cb % (ng * hpg)) // hpg, cb % hpg
            idx, dil = kind * hpg + hp, dils[g]
            if dil == 1:
                o_refs[g][idx, 0, 0] = y.astype(BF16)
            else:
                s = n_strided % N_SLABS
                n_strided += 1
                slab[s] = y
                for p in range(dil):
                    o_refs[g][idx, 0, p] = slab[s, pl.ds(p, tm // dil, stride=dil), :].astype(BF16)


def _qkv_call(tk, x, g, mod3, shift_j, scale_j, w, q_gain, k_gain, cos, sin, kinds, rope, dils=None):
    n, d = x.shape
    ncols = w.shape[1]
    ncb = ncols // LANES
    tm = ROW_TILE
    if dils is None:
        out_specs = pl.BlockSpec((ncb, tm, LANES), lambda i: (0, i, 0))
        out_shape = jax.ShapeDtypeStruct((ncb, n, LANES), BF16)
        scratch = [pltpu.VMEM((tm, d), BF16)]
    else:
        tpc = CHUNK // tm
        gcb = ncb // len(dils)
        out_specs = [pl.BlockSpec((gcb, 1, dil, tm // dil, LANES),
                                  lambda i: (0, i // tpc, 0, i % tpc, 0)) for dil in dils]
        out_shape = [jax.ShapeDtypeStruct((gcb, n // CHUNK, dil, CHUNK // dil, LANES), BF16)
                     for dil in dils]
        scratch = [pltpu.VMEM((tm, d), BF16), pltpu.VMEM((N_SLABS, tm, LANES), F32)]
    assert all(kinds[2 * c] == kinds[2 * c + 1] for c in range(ncb // 2))
    ones = np.kron(np.eye(2 * LANES // HEAD_DIM), np.ones((HEAD_DIM, HEAD_DIM))).astype(np.float32)
    e = jnp.asarray(ones, BF16)
    scale = HEAD_DIM ** -0.5
    qg = jnp.tile(q_gain.reshape(1, HEAD_DIM), (1, LANES // HEAD_DIM)) * scale
    kg = jnp.tile(k_gain.reshape(1, HEAD_DIM), (1, LANES // HEAD_DIM))
    brow = lambda i: tk.batch_row(i * tm)
    prow = lambda i: tk.pos(i * tm) // tm
    kern = functools.partial(_qkv_kernel, kinds=kinds, rope=rope, dils=dils)
    return pl.pallas_call(
        kern,
        grid=(n // tm,),
        in_specs=[
            pl.BlockSpec((tm, d), lambda i: (i, 0)),
            pl.BlockSpec((1, d), lambda i: (0, 0)),
            pl.BlockSpec((1, 1, d), lambda i: (brow(i), 0, shift_j)),
            pl.BlockSpec((1, 1, d), lambda i: (brow(i), 0, scale_j)),
            pl.BlockSpec((d, ncols), lambda i: (0, 0)),
            pl.BlockSpec((1, LANES), lambda i: (0, 0)),
            pl.BlockSpec((1, LANES), lambda i: (0, 0)),
            pl.BlockSpec((tm, LANES), lambda i: (prow(i), 0)),
            pl.BlockSpec((tm, LANES), lambda i: (prow(i), 0)),
            pl.BlockSpec((2 * LANES, 2 * LANES), lambda i: (0, 0)),
        ],
        out_specs=out_specs,
        out_shape=out_shape,
        scratch_shapes=scratch,
        compiler_params=_cparams(("arbitrary",)),
        name="norm_qkv",
    )(x, g, mod3, mod3, w, qg, kg, cos, sin, e)


def _rope_tables(max_s):
    half = HEAD_DIM // 2
    inv = ROPE_THETA ** (-np.arange(half, dtype=np.float64) / half)
    ang = np.arange(max_s, dtype=np.float64)[:, None] * inv[None, :]
    cos = np.cos(ang)
    sin = np.sin(ang)
    cos_head = np.concatenate([cos, cos], axis=1)
    sin_head = np.concatenate([-sin, sin], axis=1)
    reps = LANES // HEAD_DIM
    return (jnp.asarray(np.tile(cos_head, (1, reps)), F32),
            jnp.asarray(np.tile(sin_head, (1, reps)), F32))


def _band_kernel(q_ref, kp_ref, kc_ref, kn_ref, vp_ref, vc_ref, vn_ref, o_ref, lse_ref, *,
                 lc, chunks_p, chunks_per_seq_p):
    c = pl.program_id(0)
    i = pl.program_id(2)
    tq = BAND_TQ
    r = BAND_RADIUS
    kw = tq + 2 * r
    in_prompt = c < chunks_p
    cofs = jnp.where(in_prompt, c % chunks_per_seq_p, 0)
    seq_rows = jnp.where(in_prompt, chunks_per_seq_p, 1) * lc
    u0 = cofs * lc + i * tq
    row = lax.broadcasted_iota(jnp.int32, (tq, kw), 0)
    col = lax.broadcasted_iota(jnp.int32, (tq, kw), 1)
    uk = u0 - r + col
    valid = (jnp.abs(col - r - row) <= r) & (uk >= 0) & (uk < seq_rows)
    lane = lax.broadcasted_iota(jnp.int32, (1, LANES), 1)
    lo = lane < HEAD_DIM
    hi = jnp.logical_not(lo)
    nhp = q_ref.shape[0]
    qs, ks, vs = [], [], []
    for hp in range(nhp):
        q = q_ref[hp, 0, 0]
        k = jnp.concatenate([kp_ref[hp, 0, 0], kc_ref[hp, 0, 0], kn_ref[hp, 0, 0]], axis=0)
        v = jnp.concatenate([vp_ref[hp, 0, 0], vc_ref[hp, 0, 0], vn_ref[hp, 0, 0]], axis=0)
        qs += [jnp.where(lo, q, jnp.zeros_like(q)), jnp.where(hi, q, jnp.zeros_like(q))]
        ks += [k, k]
        vs += [v, v]
    s = lax.dot_general(jnp.stack(qs), jnp.stack(ks), (((2,), (2,)), ((0,), (0,))),
                        preferred_element_type=F32)
    s = jnp.where(valid[None], s, NEG_INF)
    m = jnp.max(s, axis=-1, keepdims=True)
    p = jnp.exp(s - m)
    den = jnp.sum(p, axis=-1, keepdims=True)
    o = lax.dot_general(p.astype(BF16), jnp.stack(vs), (((2,), (1,)), ((0,), (0,))),
                        preferred_element_type=F32) / den
    lse = m + jnp.log(den)
    for hp in range(nhp):
        o_ref[hp, 0, 0] = jnp.where(lo, o[2 * hp], o[2 * hp + 1]).astype(BF16)
        lse_ref[hp, 0, 0] = jnp.where(lo, lse[2 * hp], lse[2 * hp + 1])


def _band_call(tk, arr, q_blk, k_blk, v_blk, dil):
    nch = tk.n_chunks
    lc = CHUNK // dil
    tq = BAND_TQ
    r = BAND_RADIUS
    hpg = HEADS_PER_GROUP_A // 2
    per = tq // r
    nb = lc // r

    def cur(blk):
        return pl.BlockSpec((hpg, 1, 1, tq, LANES), lambda c, p, i: (blk, c, p, i, 0))

    def prev(blk):
        def imap(c, p, i):
            j = per * i - 1
            return (blk, jnp.where(j < 0, jnp.maximum(c - 1, 0), c), p, jnp.where(j < 0, nb - 1, j), 0)
        return pl.BlockSpec((hpg, 1, 1, r, LANES), imap)

    def nxt(blk):
        def imap(c, p, i):
            j = per * (i + 1)
            return (blk, jnp.where(j >= nb, jnp.minimum(c + 1, nch - 1), c), p,
                    jnp.where(j >= nb, 0, j), 0)
        return pl.BlockSpec((hpg, 1, 1, r, LANES), imap)

    kern = functools.partial(_band_kernel, lc=lc, chunks_p=tk.n_p // CHUNK,
                             chunks_per_seq_p=tk.s_p // CHUNK)
    out_spec = pl.BlockSpec((hpg, 1, 1, tq, LANES), lambda c, p, i: (0, c, p, i, 0))
    return pl.pallas_call(
        kern,
        grid=(nch, dil, lc // tq),
        in_specs=[cur(q_blk), prev(k_blk), cur(k_blk), nxt(k_blk), prev(v_blk), cur(v_blk), nxt(v_blk)],
        out_specs=[out_spec, out_spec],
        out_shape=[jax.ShapeDtypeStruct((hpg, nch, dil, lc, LANES), BF16),
                   jax.ShapeDtypeStruct((hpg, nch, dil, lc, LANES), F32)],
        compiler_params=_cparams(("arbitrary", "arbitrary", "arbitrary")),
        name="band_attention",
    )(arr, arr, arr, arr, arr, arr, arr)


def _dilated_attention(tk, qkv_groups):
    return [_band_call(tk, arr, 0, 1, 2, dil) for arr, dil in zip(qkv_groups, DILATIONS)]


def _na_kernel(q_ref, kp_ref, kc_ref, kn_ref, vp_ref, vc_ref, vn_ref, bias_ref, o_ref, kbuf, vbuf, *,
               rows_p, rows_s, blocks_p):
    bq = pl.program_id(1)
    tq = NA_TQ
    rows_per_blk = tq // GRID_W
    in_prompt = bq < blocks_p
    rows = jnp.where(in_prompt, rows_p, rows_s)
    blk_in_seq = jnp.where(in_prompt, bq % (rows_p // rows_per_blk),
                           (bq - blocks_p) % (rows_s // rows_per_blk))
    r0 = blk_in_seq * rows_per_blk
    kbuf[0 * tq:1 * tq] = kp_ref[0]
    kbuf[1 * tq:2 * tq] = kc_ref[0]
    kbuf[2 * tq:3 * tq] = kn_ref[0]
    vbuf[0 * tq:1 * tq] = vp_ref[0]
    vbuf[1 * tq:2 * tq] = vc_ref[0]
    vbuf[2 * tq:3 * tq] = vn_ref[0]
    lane = lax.broadcasted_iota(jnp.int32, (1, LANES), 1)
    lo = lane < HEAD_DIM
    hi = jnp.logical_not(lo)
    nkeys = NA_ROWS * GRID_W
    qs, ks, vs, bs = [], [], [], []
    for d in range(rows_per_blk):
        r = r0 + d
        rs = jnp.clip(r - NA_ROWS // 2, 0, rows - NA_ROWS)
        delta = r - rs
        off = pl.multiple_of((rs - (r0 - rows_per_blk)) * GRID_W, GRID_W)
        k = kbuf[pl.ds(off, nkeys), :]
        v = vbuf[pl.ds(off, nkeys), :]
        q = q_ref[0, d * GRID_W:(d + 1) * GRID_W, :]
        qs += [jnp.where(lo, q, jnp.zeros_like(q)), jnp.where(hi, q, jnp.zeros_like(q))]
        ks += [k, k]
        vs += [v, v]
        bs += [bias_ref[0, 0, delta], bias_ref[0, 1, delta]]
    s = lax.dot_general(jnp.stack(qs), jnp.stack(ks), (((2,), (2,)), ((0,), (0,))),
                        preferred_element_type=F32)
    s = s + jnp.stack(bs)
    m = jnp.max(s, axis=-1, keepdims=True)
    p = jnp.exp(s - m)
    den = jnp.sum(p, axis=-1, keepdims=True)
    o = lax.dot_general(p.astype(BF16), jnp.stack(vs), (((2,), (1,)), ((0,), (0,))),
                        preferred_element_type=F32) / den
    for d in range(rows_per_blk):
        o_ref[0, d * GRID_W:(d + 1) * GRID_W, :] = jnp.where(lo, o[2 * d], o[2 * d + 1]).astype(BF16)


def _na_bias_table(rpb):
    qc = np.arange(GRID_W)
    cstart = np.clip(qc - NA_COLS // 2, 0, GRID_W - NA_COLS)
    kc = np.arange(GRID_W)
    col_valid = (kc[None, :] >= cstart[:, None]) & (kc[None, :] < cstart[:, None] + NA_COLS)
    col_off = np.clip(kc[None, :] - qc[:, None] + NA_COLS - 1, 0, 2 * NA_COLS - 2)
    delta = np.arange(NA_ROWS)
    row_off = np.arange(NA_ROWS)[None, :] - delta[:, None] + NA_ROWS - 1
    t = rpb[:, row_off]
    t = t[:, :, :, col_off]
    t = jnp.where(jnp.asarray(col_valid)[None, None, None], t, NEG_INF)
    t = jnp.transpose(t, (0, 1, 3, 2, 4))
    h = rpb.shape[0]
    return t.reshape(h // 2, 2, NA_ROWS, GRID_W, NA_ROWS * GRID_W).astype(F32)


def _na_call(tk, qkv, bias):
    n = tk.n
    tq = NA_TQ
    nhp = N_HEADS_B // 2
    nblk = n // tq
    arr = qkv.reshape(3 * nhp, nblk, tq, LANES)
    nkeys = NA_ROWS * GRID_W

    arr3 = arr
    kern = functools.partial(_na_kernel, rows_p=tk.s_p // GRID_W, rows_s=tk.s_s // GRID_W,
                             blocks_p=tk.n_p // tq)

    def bspec(base, shift):
        return pl.BlockSpec((None, 1, tq, LANES),
                            lambda hp, b: (base + hp, jnp.clip(b + shift, 0, nblk - 1), 0, 0))

    return pl.pallas_call(
        kern,
        grid=(nhp, nblk),
        in_specs=[bspec(0, 0), bspec(nhp, -1), bspec(nhp, 0), bspec(nhp, 1),
                  bspec(2 * nhp, -1), bspec(2 * nhp, 0), bspec(2 * nhp, 1),
                  pl.BlockSpec((1, 2, NA_ROWS, GRID_W, nkeys), lambda hp, b: (hp, 0, 0, 0, 0))],
        out_specs=pl.BlockSpec((None, 1, tq, LANES), lambda hp, b: (hp, b, 0, 0)),
        out_shape=jax.ShapeDtypeStruct((nhp, nblk, tq, LANES), BF16),
        scratch_shapes=[pltpu.VMEM((3 * tq, LANES), BF16), pltpu.VMEM((3 * tq, LANES), BF16)],
        compiler_params=_cparams(("arbitrary", "arbitrary")),
        name="neighbourhood_attention",
    )(arr3, arr3, arr3, arr3, arr3, arr3, arr3, bias).reshape(nhp, n, LANES)


def _oproj_kernel(*refs, dils, n_tiles):
    n_groups = 1 if dils is None else len(dils)
    o_refs = refs[:n_groups]
    l_refs = refs[n_groups:2 * n_groups] if dils is not None else ()
    k = len(o_refs) + len(l_refs)
    (x_ref, wo_ref, g1_ref, ng_ref, sh_ref, sc_ref, wrh_ref, wrl_ref, br_ref,
     xo_ref, h_ref, lg_ref, m_ref) = refs[k:k + 13]
    slab = refs[k + 13] if dils is not None else None
    tm = x_ref.shape[0]
    n_strided = [0]

    def natural(ref, t, dil):
        if dil == 1:
            return ref[t, 0, 0].astype(F32)
        s = n_strided[0] % N_SLABS
        n_strided[0] += 1
        for p in range(dil):
            slab[s, pl.ds(p, tm // dil, stride=dil), :] = ref[t, 0, p].astype(F32)
        return slab[s]

    for t in range(n_tiles):
        if dils is None:
            m_ref[:, t * LANES:(t + 1) * LANES] = o_refs[0][t]
        else:
            ls = [natural(l, t, dil) for l, dil in zip(l_refs, dils)]
            mx = functools.reduce(jnp.maximum, ls)
            es = [jnp.exp(l - mx) for l in ls]
            den = functools.reduce(lambda a, b: a + b, es)
            num = functools.reduce(lambda a, b: a + b,
                                   [e * natural(o, t, dil) for e, o, dil in zip(es, o_refs, dils)])
            m_ref[:, t * LANES:(t + 1) * LANES] = (num / den).astype(BF16)
    mix = jnp.dot(m_ref[...], wo_ref[...], preferred_element_type=F32)
    x = x_ref[...] + g1_ref[0] * mix
    xo_ref[...] = x
    h = _norm_modulate(x, ng_ref[...], sh_ref[0], sc_ref[0])
    hi = h.astype(BF16)
    lo = (h - hi.astype(F32)).astype(BF16)
    tm, d = h.shape
    nsub = d // LANES
    for j in range(nsub):
        h_ref[pl.ds(j, tm, stride=nsub), :] = h[:, j * LANES:(j + 1) * LANES]
    lg = jnp.dot(hi, wrh_ref[...], preferred_element_type=F32)
    lg = lg + jnp.dot(lo, wrh_ref[...], preferred_element_type=F32)
    lg = lg + jnp.dot(hi, wrl_ref[...], preferred_element_type=F32)
    lg_ref[...] = lg + br_ref[...]


def _oproj_call(tk, os, ls, x, w_o, mod3, g1_j, ng, sh_j, sc_j, w_router, b_router, dils=None):
    n, d = x.shape
    tm = ROW_TILE
    n_tiles = os[0].shape[0]
    k_in = n_tiles * LANES
    if dils is None:
        tile_specs = [pl.BlockSpec((n_tiles, tm, LANES), lambda i: (0, i, 0))]
        scratch = [pltpu.VMEM((tm, k_in), BF16)]
    else:
        tpc = CHUNK // tm
        tile_specs = [pl.BlockSpec((n_tiles, 1, dil, tm // dil, LANES),
                                   lambda i: (0, i // tpc, 0, i % tpc, 0)) for dil in dils] * 2
        scratch = [pltpu.VMEM((tm, k_in), BF16), pltpu.VMEM((N_SLABS, tm, LANES), F32)]
    ne = w_router.shape[1]
    wr = jnp.zeros((d, ROUTER_PAD), F32).at[:, :ne].set(w_router)
    wr_hi = wr.astype(BF16)
    wr_lo = (wr - wr_hi.astype(F32)).astype(BF16)
    br = jnp.zeros((1, ROUTER_PAD), F32).at[0, :ne].set(b_router)
    brow = lambda i: tk.batch_row(i * tm)
    row_spec = pl.BlockSpec((tm, d), lambda i: (i, 0))
    mod_spec = lambda j: pl.BlockSpec((1, 1, d), lambda i: (brow(i), 0, j))
    kern = functools.partial(_oproj_kernel, dils=dils, n_tiles=n_tiles)
    return pl.pallas_call(
        kern,
        grid=(n // tm,),
        in_specs=tile_specs + [
            row_spec,
            pl.BlockSpec((k_in, d), lambda i: (0, 0)),
            mod_spec(g1_j),
            pl.BlockSpec((1, d), lambda i: (0, 0)),
            mod_spec(sh_j),
            mod_spec(sc_j),
            pl.BlockSpec((d, ROUTER_PAD), lambda i: (0, 0)),
            pl.BlockSpec((d, ROUTER_PAD), lambda i: (0, 0)),
            pl.BlockSpec((1, ROUTER_PAD), lambda i: (0, 0)),
        ],
        out_specs=[row_spec, pl.BlockSpec((tm * (d // LANES), LANES), lambda i: (i, 0)),
                   pl.BlockSpec((tm, ROUTER_PAD), lambda i: (i, 0))],
        out_shape=[jax.ShapeDtypeStruct((n, d), F32), jax.ShapeDtypeStruct((n * (d // LANES), LANES), F32),
                   jax.ShapeDtypeStruct((n, ROUTER_PAD), F32)],
        scratch_shapes=scratch,
        compiler_params=_cparams(("arbitrary",)),
        name="oproj_norm_router",
    )(*os, *ls, x, w_o, mod3, ng, mod3, mod3, wr_hi, wr_lo, br)


DMA_UNROLL = 16
ISSUE_SPLIT = (7, 12, 13)


def _moe_kernel(te_ref, nu_ref, gcur_ref, gnxt_ref, sprev_ref, h_hbm, wgu_ref, bgu_ref, wdn_ref,
                bdn_ref, y_hbm, xbuf, ybuf, acc_ref, wgu_b, wdn_b, sem_g, sem_s, *, d_ff, fchunk):
    t = pl.program_id(0)
    tm, d = acc_ref.shape
    nsub = d // LANES
    half = tm // 2
    n_used = nu_ref[0]
    n_chunks = d_ff // fchunk
    assert n_chunks == 4

    def row(ref, r):
        return ref.at[pl.ds(pl.multiple_of(r * nsub, nsub), nsub)]

    def gather_row(idx_ref, r, buf, prio=0):
        pltpu.make_async_copy(row(h_hbm, idx_ref[0, 0, r]), row(xbuf.at[buf], r),
                              sem_g.at[buf]).start(priority=prio)

    def scatter_row(idx_ref, r, buf, prio=1):
        pltpu.make_async_copy(row(ybuf.at[buf], r), row(y_hbm, idx_ref[0, 0, r]),
                              sem_s.at[buf]).start(priority=prio)

    def wait_gather_half(buf):
        pltpu.make_async_copy(h_hbm.at[pl.ds(0, half * nsub)], xbuf.at[buf, pl.ds(0, half * nsub)],
                              sem_g.at[buf]).wait()

    def wait_scatter_half(buf):
        pltpu.make_async_copy(ybuf.at[buf, pl.ds(0, half * nsub)], y_hbm.at[pl.ds(0, half * nsub)],
                              sem_s.at[buf]).wait()

    def rolled(issue):
        def body(i, carry):
            for j in range(DMA_UNROLL):
                issue(i * DMA_UNROLL + j)
            return carry
        lax.fori_loop(0, tm // DMA_UNROLL, body, 0)

    @pl.when(t == 0)
    def _():
        ybuf[...] = jnp.zeros_like(ybuf)
        spare = y_hbm.shape[0] - 2 * tm * nsub
        pltpu.make_async_copy(ybuf.at[0], y_hbm.at[pl.ds(spare, tm * nsub)], sem_s.at[0]).start()
        rolled(lambda r: gather_row(gcur_ref, r, 0))
        wait_gather_half(0)

    def step(s):
        o = 1 - s

        @pl.when((t == 0) | (te_ref[t] != te_ref[jnp.maximum(t - 1, 0)]))
        def _():
            rows = 128
            def cast(i, carry):
                r0 = pl.multiple_of(i * rows, rows)
                wgu_b[pl.ds(r0, rows), :] = wgu_ref[0, 0, pl.ds(r0, rows), :].astype(BF16)
                wdn_b[pl.ds(r0, rows), :] = wdn_ref[0, 0, pl.ds(r0, rows), :].astype(BF16)
                return carry
            lax.fori_loop(0, d // rows, cast, 0)

        wait_gather_half(s)
        x = jnp.concatenate([xbuf[s, pl.ds(j, tm, stride=nsub), :] for j in range(nsub)],
                            axis=1).astype(BF16)
        issues = []
        for r in range(tm):
            issues.append(("g", r))
            if r % 2 == 1:
                issues.append(("s", r // 2))
        issues += [("s", r) for r in range(half, tm)]
        n0 = ISSUE_SPLIT[0] * 2 * tm // sum(ISSUE_SPLIT)
        n1 = n0 + ISSUE_SPLIT[1] * 2 * tm // sum(ISSUE_SPLIT)
        groups = [issues[:n0], issues[n0:n1], issues[n1:], []]
        for c in range(n_chunks):
            lo, hi = c * fchunk, (c + 1) * fchunk
            if c == 1:
                wait_scatter_half(s)
            if c == 2:
                wait_scatter_half(s)
            if c == 3:
                wait_gather_half(o)
            gate = jnp.dot(x, wgu_b[:, lo:hi], preferred_element_type=F32) + bgu_ref[0, 0, :, lo:hi]
            up = jnp.dot(x, wgu_b[:, d_ff + lo:d_ff + hi],
                         preferred_element_type=F32) + bgu_ref[0, 0, :, d_ff + lo:d_ff + hi]
            gate = jnp.minimum(gate, SWIGLU_LIMIT)
            up = jnp.clip(up, -SWIGLU_LIMIT, SWIGLU_LIMIT)
            a = ((up + 1.0) * gate * jax.nn.sigmoid(SWIGLU_ALPHA * gate)).astype(BF16)
            part = jnp.dot(a, wdn_b[lo:hi, :], preferred_element_type=F32)
            if c == 0:
                acc_ref[...] = part + bdn_ref[0, 0]
            elif c < n_chunks - 1:
                acc_ref[...] += part
            else:
                for j in range(nsub):
                    cols = slice(j * LANES, (j + 1) * LANES)
                    ybuf[s, pl.ds(j, tm, stride=nsub), :] = acc_ref[:, cols] + part[:, cols]
            for k, (kind, r) in enumerate(groups[c]):
                if kind == "g":
                    gather_row(gnxt_ref, r, o, prio=k % 2)
                else:
                    scatter_row(sprev_ref, r, o, prio=k % 2)

    def drain(s):
        o = 1 - s
        wait_gather_half(s)
        rolled(lambda r: scatter_row(sprev_ref, r, o))
        for buf in (0, 1):
            wait_scatter_half(buf)
            wait_scatter_half(buf)

    pl.when(t < n_used)(functools.partial(step, t % 2))
    pl.when(t == n_used)(functools.partial(drain, t % 2))


def _moe_call(h, route, w_gu, b_gu, w_dn, b_dn, layer):
    depth, ne, d, d_ff2 = w_gu.shape
    nsub = d // LANES
    n = h.shape[0] // nsub
    d_ff = d_ff2 // 2
    tm = MOE_TILE
    n_tiles = route["tile_expert"].shape[0]
    kern = functools.partial(_moe_kernel, d_ff=d_ff, fchunk=256)
    idx_spec = lambda shift: pl.BlockSpec(
        (1, 1, tm), lambda t, te, nu: (jnp.minimum(t + shift, n_tiles - 1), 0, 0),
        memory_space=pltpu.SMEM)
    grid_spec = pltpu.PrefetchScalarGridSpec(
        num_scalar_prefetch=2,
        grid=(n_tiles,),
        in_specs=[
            idx_spec(0), idx_spec(1), idx_spec(0),
            pl.BlockSpec(memory_space=pl.ANY),
            pl.BlockSpec((1, 1, d, d_ff2), lambda t, te, nu: (layer, te[t], 0, 0)),
            pl.BlockSpec((1, 1, 1, d_ff2), lambda t, te, nu: (layer, te[t], 0, 0)),
            pl.BlockSpec((1, 1, d_ff, d), lambda t, te, nu: (layer, te[t], 0, 0)),
            pl.BlockSpec((1, 1, 1, d), lambda t, te, nu: (layer, te[t], 0, 0)),
        ],
        out_specs=pl.BlockSpec(memory_space=pl.ANY),
        scratch_shapes=[pltpu.VMEM((2, tm * nsub, LANES), F32), pltpu.VMEM((2, tm * nsub, LANES), F32),
                        pltpu.VMEM((tm, d), F32),
                        pltpu.VMEM((d, d_ff2), BF16), pltpu.VMEM((d_ff, d), BF16),
                        pltpu.SemaphoreType.DMA((2,)), pltpu.SemaphoreType.DMA((2,))],
    )
    return pl.pallas_call(
        kern,
        grid_spec=grid_spec,
        out_shape=jax.ShapeDtypeStruct(((TOP_K * n + 2 * tm) * nsub, LANES), F32),
        compiler_params=_cparams(("arbitrary",)),
        name="moe_experts",
    )(route["tile_expert"], route["n_used"], route["gsrc"], route["gsrc"], route["sdst_prev"], h,
      w_gu, b_gu.reshape(depth, ne, 1, d_ff2), w_dn, b_dn.reshape(depth, ne, 1, d))


def _route(logits, n):
    ne = N_EXPERTS
    tm = MOE_TILE
    top_val, top_idx = lax.top_k(logits[:, :ne], TOP_K)
    gates = jax.nn.softmax(top_val, axis=-1)
    e_flat = top_idx.reshape(-1).astype(jnp.int32)
    nk = n * TOP_K
    counts = jnp.sum((e_flat[:, None] == jnp.arange(ne, dtype=jnp.int32)[None, :]).astype(jnp.int32),
                     axis=0)
    padded = (counts + tm - 1) // tm * tm
    pad_ends = jnp.cumsum(padded)
    n_tiles = nk // tm + ne
    cap = n_tiles * tm
    tile_start = jnp.arange(n_tiles, dtype=jnp.int32) * tm
    tile_expert = jnp.minimum(jnp.sum((pad_ends[None, :] <= tile_start[:, None]).astype(jnp.int32), axis=1),
                              ne - 1).astype(jnp.int32)
    n_used = (pad_ends[-1] // tm).astype(jnp.int32).reshape(1)
    idx_bits = int(np.ceil(np.log2(nk + tm)))
    pad_base = (1 << idx_bits) - tm
    real_keys = (e_flat << idx_bits) + jnp.arange(nk, dtype=jnp.int32)
    lane = jnp.arange(tm, dtype=jnp.int32)[None, :]
    expert = jnp.arange(ne, dtype=jnp.int32)[:, None]
    pad_keys = jnp.where(lane < (padded - counts)[:, None], (expert << idx_bits) + pad_base + lane,
                         jnp.iinfo(jnp.int32).max)
    keys = jnp.concatenate([real_keys, pad_keys.reshape(-1)])
    low = lax.sort(keys, is_stable=False) & ((1 << idx_bits) - 1)
    valid = low < pad_base
    slot = jnp.arange(cap, dtype=jnp.int32)
    gsrc = jnp.where(valid, low // TOP_K, 0).astype(jnp.int32)
    spare = nk + ((slot // tm) % 2) * tm + slot % tm
    sdst = jnp.where(valid, (low % TOP_K) * n + low // TOP_K, spare).astype(jnp.int32)
    sdst = sdst.reshape(n_tiles, 1, tm)
    before_first = (nk + tm + jnp.arange(tm, dtype=jnp.int32)).reshape(1, 1, tm)
    sdst_prev = jnp.concatenate([before_first, sdst[:-1]], axis=0)
    return dict(gates=gates, tile_expert=tile_expert, n_used=n_used,
                gsrc=gsrc.reshape(n_tiles, 1, tm), sdst_prev=sdst_prev)


def _combine_kernel(x_ref, y0_ref, y1_ref, y2_ref, y3_ref, gt_ref, g2_ref, o_ref):
    tm, d = x_ref.shape
    nsub = d // LANES
    g = gt_ref[...]
    for j in range(nsub):
        y = None
        for k, y_ref in enumerate((y0_ref, y1_ref, y2_ref, y3_ref)):
            term = g[:, k:k + 1] * y_ref[pl.ds(j, tm, stride=nsub), :]
            y = term if y is None else y + term
        cols = slice(j * LANES, (j + 1) * LANES)
        o_ref[:, cols] = x_ref[:, cols] + g2_ref[0, :, cols] * y


def _combine_call(tk, x, ycomb, gates, mod3, g2_j):
    n, d = x.shape
    nsub = d // LANES
    tm = COMBINE_TILE
    nt = n // tm
    brow = lambda i: tk.batch_row(i * tm)
    row_spec = pl.BlockSpec((tm, d), lambda i: (i, 0))
    y_spec = lambda k: pl.BlockSpec((tm * nsub, LANES), lambda i: (k * nt + i, 0))
    return pl.pallas_call(
        _combine_kernel,
        grid=(nt,),
        in_specs=[row_spec, y_spec(0), y_spec(1), y_spec(2), y_spec(3),
                  pl.BlockSpec((tm, TOP_K), lambda i: (i, 0)),
                  pl.BlockSpec((1, 1, d), lambda i: (brow(i), 0, g2_j))],
        out_specs=row_spec,
        out_shape=jax.ShapeDtypeStruct((n, d), F32),
        compiler_params=_cparams(("arbitrary",)),
        name="moe_combine",
    )(x, ycomb, ycomb, ycomb, ycomb, gates, mod3)


def _moe_experts(tk, h, logits, w_gu, b_gu, w_dn, b_dn, layer):
    route = _route(logits, logits.shape[0])
    return _moe_call(h, route, w_gu, b_gu, w_dn, b_dn, layer), route["gates"]


@jax.jit
def _forward(x_prompt, x_sample, c_prompt, c_sample, norm_g, w_mod, b_mod, a_w_qkv, a_q_gain,
             a_k_gain, a_w_o, b_w_qkv, b_q_gain, b_k_gain, b_rpb, b_w_o, w_router, b_router,
             w_gu, b_gu, w_dn, b_dn):
    b_p, s_p, d = x_prompt.shape
    b_s, s_s, _ = x_sample.shape
    tk = _Tokens(b_p, s_p, b_s, s_s)
    depth = w_mod.shape[0]
    x = jnp.concatenate([x_prompt.reshape(-1, d), x_sample.reshape(-1, d)], axis=0)

    nb = b_p + b_s
    rows = -(-nb // 8) * 8
    c_all = jnp.zeros((rows, d), F32).at[:nb].set(jnp.concatenate([c_prompt, c_sample], axis=0))
    mods = _modulation(c_all, w_mod, b_mod)

    cos, sin = _rope_tables(tk.max_s)
    kinds_a = tuple("qkv"[cb // (len(DILATIONS) * HEADS_PER_GROUP_A // 2)]
                    for cb in range(3 * len(DILATIONS) * HEADS_PER_GROUP_A // 2))
    kinds_b = tuple("qkv"[cb // (N_HEADS_B // 2)] for cb in range(3 * N_HEADS_B // 2))

    for layer in range(depth):
        mod3 = mods[layer].reshape(rows, 1, 6 * d)
        j = layer // 2
        if layer % 2 == 0:
            qkv = _qkv_call(tk, x, norm_g[layer, 0].reshape(1, d), mod3, 0, 1,
                            a_w_qkv[j].astype(BF16), a_q_gain[j], a_k_gain[j], cos, sin, kinds_a, True,
                            dils=DILATIONS)
            groups = _dilated_attention(tk, qkv)
            os = [g[0] for g in groups]
            ls = [g[1] for g in groups]
            w_o = a_w_o[j]
            dils = DILATIONS
        else:
            dils = None
            qkv = _qkv_call(tk, x, norm_g[layer, 0].reshape(1, d), mod3, 0, 1,
                            b_w_qkv[j].astype(BF16), b_q_gain[j], b_k_gain[j], cos, sin, kinds_b, False)
            os = [_na_call(tk, qkv, _na_bias_table(b_rpb[j]))]
            ls = []
            w_o = b_w_o[j]
        x, hff, logits = _oproj_call(tk, os, ls, x, w_o.astype(BF16), mod3, 2,
                                     norm_g[layer, 1].reshape(1, d), 3, 4,
                                     w_router[layer], b_router[layer], dils=dils)
        ycomb, gates = _moe_experts(tk, hff, logits, w_gu, b_gu, w_dn, b_dn, layer)
        x = _combine_call(tk, x, ycomb, gates, mod3, 5)

    y_prompt = x[:tk.n_p].reshape(b_p, s_p, d)
    y_sample = x[tk.n_p:].reshape(b_s, s_s, d)
    return y_prompt, y_sample


def kernel(x_prompt, x_sample, c_prompt, c_sample, norm_g, w_mod, b_mod, a_w_qkv, a_q_gain, a_k_gain,
           a_w_o, b_w_qkv, b_q_gain, b_k_gain, b_rpb, b_w_o, w_router, b_router, w_gu, b_gu, w_dn, b_dn):
    return _forward(x_prompt, x_sample, c_prompt, c_sample, norm_g, w_mod, b_mod, a_w_qkv, a_q_gain,
                    a_k_gain, a_w_o, b_w_qkv, b_q_gain, b_k_gain, b_rpb, b_w_o, w_router, b_router,
                    w_gu, b_gu, w_dn, b_dn)
```

```python
import functools

import numpy as np
import jax
import jax.numpy as jnp
from jax import lax
from jax.experimental import pallas as pl
from jax.experimental.pallas import tpu as pltpu

F32 = jnp.float32
BF16 = jnp.bfloat16

HEAD_DIM = 64
ROPE_THETA = 10000.0
NORM_EPS = 1e-6
NEG_INF = -1e30
DILATIONS = (1, 4, 16)
BAND_RADIUS = 64
HEADS_PER_GROUP_A = 8
N_HEADS_B = 16
GRID_W = 64
NA_ROWS = 8
NA_COLS = 16
N_EXPERTS = 32
TOP_K = 4
SWIGLU_LIMIT = 7.0
SWIGLU_ALPHA = 1.702

LANES = 128
VMEM_LIMIT_BYTES = 56 * 1024 * 1024

CHUNK = 2048
ROW_TILE = 512
BAND_TQ = 128
NA_TQ = 1024
MOE_TILE = 512
COMBINE_TILE = 256
ROUTER_PAD = LANES


def _cparams(sem):
    return pltpu.CompilerParams(dimension_semantics=sem, vmem_limit_bytes=VMEM_LIMIT_BYTES)


class _Tokens:
    def __init__(self, b_p, s_p, b_s, s_s):
        assert s_p % CHUNK == 0 and s_s % CHUNK == 0
        self.b_p, self.s_p, self.b_s, self.s_s = b_p, s_p, b_s, s_s
        self.n_p = b_p * s_p
        self.n = self.n_p + b_s * s_s
        self.n_chunks = self.n // CHUNK
        self.max_s = max(s_p, s_s)

    def batch_row(self, t0):
        return jnp.where(t0 < self.n_p, t0 // self.s_p, self.b_p + (t0 - self.n_p) // self.s_s)

    def pos(self, t0):
        return jnp.where(t0 < self.n_p, t0 % self.s_p, (t0 - self.n_p) % self.s_s)

    def seq_len(self, t0):
        return jnp.where(t0 < self.n_p, self.s_p, self.s_s)


def _mod_kernel(c_ref, w_ref, b_ref, o_ref):
    c = c_ref[...]
    act = c * jax.nn.sigmoid(c)
    o_ref[0] = jnp.dot(act, w_ref[0], precision=lax.Precision.HIGHEST,
                       preferred_element_type=F32) + b_ref[0]


def _modulation(c_all, w_mod, b_mod):
    depth, d, d6 = w_mod.shape
    rows = c_all.shape[0]
    return pl.pallas_call(
        _mod_kernel,
        grid=(depth, d6 // d),
        in_specs=[
            pl.BlockSpec((rows, d), lambda l, j: (0, 0)),
            pl.BlockSpec((1, d, d), lambda l, j: (l, 0, j)),
            pl.BlockSpec((1, 1, d), lambda l, j: (l, 0, j)),
        ],
        out_specs=pl.BlockSpec((1, rows, d), lambda l, j: (l, 0, j)),
        out_shape=jax.ShapeDtypeStruct((depth, rows, d6), F32),
        compiler_params=_cparams(("arbitrary", "arbitrary")),
        name="modulation",
    )(c_all, w_mod, b_mod.reshape(depth, 1, d6))


def _norm_modulate(x, g, shift, scale):
    y = x * lax.rsqrt(jnp.mean(x * x, axis=-1, keepdims=True) + NORM_EPS)
    return (y * g) * (1.0 + scale) + shift


N_SLABS = 4


def _qkv_kernel(x_ref, g_ref, sh_ref, sc_ref, w_ref, qg_ref, kg_ref, cos_ref, sin_ref, e_ref,
                *rest, kinds, rope, dils):
    n_out = 1 if dils is None else len(dils)
    o_refs, h_ref = rest[:n_out], rest[n_out]
    slab = None if dils is None else rest[n_out + 1]
    tm = x_ref.shape[0]
    hpg = HEADS_PER_GROUP_A // 2
    h_ref[...] = _norm_modulate(x_ref[...], g_ref[...], sh_ref[0], sc_ref[0]).astype(BF16)
    lane = lax.broadcasted_iota(jnp.int32, (1, LANES), 1)
    first_half = (lane % HEAD_DIM) < (HEAD_DIM // 2)
    n_strided = 0
    wide = None
    for c in range(len(kinds) // 2):
        if c % 2 == 0:
            wide = jnp.dot(h_ref[...], w_ref[:, c * 2 * LANES:(c + 2) * 2 * LANES],
                           preferred_element_type=F32)
        acc = wide[:, (c % 2) * 2 * LANES:(c % 2 + 1) * 2 * LANES]
        if kinds[2 * c] != "v":
            ss = jnp.dot((acc * acc).astype(BF16), e_ref[...], preferred_element_type=F32)
            acc = acc * lax.rsqrt(ss * (1.0 / HEAD_DIM) + NORM_EPS)
        for half in range(2):
            cb = 2 * c + half
            y = acc[:, half * LANES:(half + 1) * LANES]
            if kinds[cb] != "v":
                y = y * (qg_ref[...] if kinds[cb] == "q" else kg_ref[...])
                if rope:
                    partner = jnp.where(first_half, pltpu.roll(y, LANES - HEAD_DIM // 2, 1),
                                        pltpu.roll(y, HEAD_DIM // 2, 1))
                    y = y * cos_ref[...] + partner * sin_ref[...]
            if dils is None:
                o_refs[0][cb] = y.astype(BF16)
                continue
            ng = len(dils)
            kind, g, hp = cb // (ng * hpg), (cb % (ng * hpg)) // hpg, cb % hpg
            idx, dil = kind * hpg + hp, dils[g]
            if dil == 1:
                o_refs[g][idx, 0, 0] = y.astype(BF16)
            else:
                s = n_strided % N_SLABS
                n_strided += 1
                slab[s] = y
                for p in range(dil):
                    o_refs[g][idx, 0, p] = slab[s, pl.ds(p, tm // dil, stride=dil), :].astype(BF16)


def _qkv_call(tk, x, g, mod3, shift_j, scale_j, w, q_gain, k_gain, cos, sin, kinds, rope, dils=None):
    n, d = x.shape
    ncols = w.shape[1]
    ncb = ncols // LANES
    tm = ROW_TILE
    if dils is None:
        out_specs = pl.BlockSpec((ncb, tm, LANES), lambda i: (0, i, 0))
        out_shape = jax.ShapeDtypeStruct((ncb, n, LANES), BF16)
        scratch = [pltpu.VMEM((tm, d), BF16)]
    else:
        tpc = CHUNK // tm
        gcb = ncb // len(dils)
        out_specs = [pl.BlockSpec((gcb, 1, dil, tm // dil, LANES),
                                  lambda i: (0, i // tpc, 0, i % tpc, 0)) for dil in dils]
        out_shape = [jax.ShapeDtypeStruct((gcb, n // CHUNK, dil, CHUNK // dil, LANES), BF16)
                     for dil in dils]
        scratch = [pltpu.VMEM((tm, d), BF16), pltpu.VMEM((N_SLABS, tm, LANES), F32)]
    assert all(kinds[2 * c] == kinds[2 * c + 1] for c in range(ncb // 2))
    ones = np.kron(np.eye(2 * LANES // HEAD_DIM), np.ones((HEAD_DIM, HEAD_DIM))).astype(np.float32)
    e = jnp.asarray(ones, BF16)
    scale = HEAD_DIM ** -0.5
    qg = jnp.tile(q_gain.reshape(1, HEAD_DIM), (1, LANES // HEAD_DIM)) * scale
    kg = jnp.tile(k_gain.reshape(1, HEAD_DIM), (1, LANES // HEAD_DIM))
    brow = lambda i: tk.batch_row(i * tm)
    prow = lambda i: tk.pos(i * tm) // tm
    kern = functools.partial(_qkv_kernel, kinds=kinds, rope=rope, dils=dils)
    return pl.pallas_call(
        kern,
        grid=(n // tm,),
        in_specs=[
            pl.BlockSpec((tm, d), lambda i: (i, 0)),
            pl.BlockSpec((1, d), lambda i: (0, 0)),
            pl.BlockSpec((1, 1, d), lambda i: (brow(i), 0, shift_j)),
            pl.BlockSpec((1, 1, d), lambda i: (brow(i), 0, scale_j)),
            pl.BlockSpec((d, ncols), lambda i: (0, 0)),
            pl.BlockSpec((1, LANES), lambda i: (0, 0)),
            pl.BlockSpec((1, LANES), lambda i: (0, 0)),
            pl.BlockSpec((tm, LANES), lambda i: (prow(i), 0)),
            pl.BlockSpec((tm, LANES), lambda i: (prow(i), 0)),
            pl.BlockSpec((2 * LANES, 2 * LANES), lambda i: (0, 0)),
        ],
        out_specs=out_specs,
        out_shape=out_shape,
        scratch_shapes=scratch,
        compiler_params=_cparams(("arbitrary",)),
        name="norm_qkv",
    )(x, g, mod3, mod3, w, qg, kg, cos, sin, e)


def _rope_tables(max_s):
    half = HEAD_DIM // 2
    inv = ROPE_THETA ** (-np.arange(half, dtype=np.float64) / half)
    ang = np.arange(max_s, dtype=np.float64)[:, None] * inv[None, :]
    cos = np.cos(ang)
    sin = np.sin(ang)
    cos_head = np.concatenate([cos, cos], axis=1)
    sin_head = np.concatenate([-sin, sin], axis=1)
    reps = LANES // HEAD_DIM
    return (jnp.asarray(np.tile(cos_head, (1, reps)), F32),
            jnp.asarray(np.tile(sin_head, (1, reps)), F32))


def _band_kernel(q_ref, kp_ref, kc_ref, kn_ref, vp_ref, vc_ref, vn_ref, o_ref, lse_ref, *,
                 lc, chunks_p, chunks_per_seq_p):
    c = pl.program_id(0)
    i = pl.program_id(2)
    tq = BAND_TQ
    r = BAND_RADIUS
    kw = tq + 2 * r
    in_prompt = c < chunks_p
    cofs = jnp.where(in_prompt, c % chunks_per_seq_p, 0)
    seq_rows = jnp.where(in_prompt, chunks_per_seq_p, 1) * lc
    u0 = cofs * lc + i * tq
    row = lax.broadcasted_iota(jnp.int32, (tq, kw), 0)
    col = lax.broadcasted_iota(jnp.int32, (tq, kw), 1)
    uk = u0 - r + col
    valid = (jnp.abs(col - r - row) <= r) & (uk >= 0) & (uk < seq_rows)
    lane = lax.broadcasted_iota(jnp.int32, (1, LANES), 1)
    lo = lane < HEAD_DIM
    hi = jnp.logical_not(lo)
    nhp = q_ref.shape[0]
    qs, ks, vs = [], [], []
    for hp in range(nhp):
        q = q_ref[hp, 0, 0]
        k = jnp.concatenate([kp_ref[hp, 0, 0], kc_ref[hp, 0, 0], kn_ref[hp, 0, 0]], axis=0)
        v = jnp.concatenate([vp_ref[hp, 0, 0], vc_ref[hp, 0, 0], vn_ref[hp, 0, 0]], axis=0)
        qs += [jnp.where(lo, q, jnp.zeros_like(q)), jnp.where(hi, q, jnp.zeros_like(q))]
        ks += [k, k]
        vs += [v, v]
    s = lax.dot_general(jnp.stack(qs), jnp.stack(ks), (((2,), (2,)), ((0,), (0,))),
                        preferred_element_type=F32)
    s = jnp.where(valid[None], s, NEG_INF)
    m = jnp.max(s, axis=-1, keepdims=True)
    p = jnp.exp(s - m)
    den = jnp.sum(p, axis=-1, keepdims=True)
    o = lax.dot_general(p.astype(BF16), jnp.stack(vs), (((2,), (1,)), ((0,), (0,))),
                        preferred_element_type=F32) / den
    lse = m + jnp.log(den)
    for hp in range(nhp):
        o_ref[hp, 0, 0] = jnp.where(lo, o[2 * hp], o[2 * hp + 1]).astype(BF16)
        lse_ref[hp, 0, 0] = jnp.where(lo, lse[2 * hp], lse[2 * hp + 1])


def _band_call(tk, arr, q_blk, k_blk, v_blk, dil):
    nch = tk.n_chunks
    lc = CHUNK // dil
    tq = BAND_TQ
    r = BAND_RADIUS
    hpg = HEADS_PER_GROUP_A // 2
    per = tq // r
    nb = lc // r

    def cur(blk):
        return pl.BlockSpec((hpg, 1, 1, tq, LANES), lambda c, p, i: (blk, c, p, i, 0))

    def prev(blk):
        def imap(c, p, i):
            j = per * i - 1
            return (blk, jnp.where(j < 0, jnp.maximum(c - 1, 0), c), p, jnp.where(j < 0, nb - 1, j), 0)
        return pl.BlockSpec((hpg, 1, 1, r, LANES), imap)

    def nxt(blk):
        def imap(c, p, i):
            j = per * (i + 1)
            return (blk, jnp.where(j >= nb, jnp.minimum(c + 1, nch - 1), c), p,
                    jnp.where(j >= nb, 0, j), 0)
        return pl.BlockSpec((hpg, 1, 1, r, LANES), imap)

    kern = functools.partial(_band_kernel, lc=lc, chunks_p=tk.n_p // CHUNK,
                             chunks_per_seq_p=tk.s_p // CHUNK)
    out_spec = pl.BlockSpec((hpg, 1, 1, tq, LANES), lambda c, p, i: (0, c, p, i, 0))
    return pl.pallas_call(
        kern,
        grid=(nch, dil, lc // tq),
        in_specs=[cur(q_blk), prev(k_blk), cur(k_blk), nxt(k_blk), prev(v_blk), cur(v_blk), nxt(v_blk)],
        out_specs=[out_spec, out_spec],
        out_shape=[jax.ShapeDtypeStruct((hpg, nch, dil, lc, LANES), BF16),
                   jax.ShapeDtypeStruct((hpg, nch, dil, lc, LANES), F32)],
        compiler_params=_cparams(("arbitrary", "arbitrary", "arbitrary")),
        name="band_attention",
    )(arr, arr, arr, arr, arr, arr, arr)


def _dilated_attention(tk, qkv_groups):
    return [_band_call(tk, arr, 0, 1, 2, dil) for arr, dil in zip(qkv_groups, DILATIONS)]


def _na_kernel(q_ref, kp_ref, kc_ref, kn_ref, vp_ref, vc_ref, vn_ref, bias_ref, o_ref, kbuf, vbuf, *,
               rows_p, rows_s, blocks_p):
    bq = pl.program_id(1)
    tq = NA_TQ
    rows_per_blk = tq // GRID_W
    in_prompt = bq < blocks_p
    rows = jnp.where(in_prompt, rows_p, rows_s)
    blk_in_seq = jnp.where(in_prompt, bq % (rows_p // rows_per_blk),
                           (bq - blocks_p) % (rows_s // rows_per_blk))
    r0 = blk_in_seq * rows_per_blk
    kbuf[0 * tq:1 * tq] = kp_ref[0]
    kbuf[1 * tq:2 * tq] = kc_ref[0]
    kbuf[2 * tq:3 * tq] = kn_ref[0]
    vbuf[0 * tq:1 * tq] = vp_ref[0]
    vbuf[1 * tq:2 * tq] = vc_ref[0]
    vbuf[2 * tq:3 * tq] = vn_ref[0]
    lane = lax.broadcasted_iota(jnp.int32, (1, LANES), 1)
    lo = lane < HEAD_DIM
    hi = jnp.logical_not(lo)
    nkeys = NA_ROWS * GRID_W
    qs, ks, vs, bs = [], [], [], []
    for d in range(rows_per_blk):
        r = r0 + d
        rs = jnp.clip(r - NA_ROWS // 2, 0, rows - NA_ROWS)
        delta = r - rs
        off = pl.multiple_of((rs - (r0 - rows_per_blk)) * GRID_W, GRID_W)
        k = kbuf[pl.ds(off, nkeys), :]
        v = vbuf[pl.ds(off, nkeys), :]
        q = q_ref[0, d * GRID_W:(d + 1) * GRID_W, :]
        qs += [jnp.where(lo, q, jnp.zeros_like(q)), jnp.where(hi, q, jnp.zeros_like(q))]
        ks += [k, k]
        vs += [v, v]
        bs += [bias_ref[0, 0, delta], bias_ref[0, 1, delta]]
    s = lax.dot_general(jnp.stack(qs), jnp.stack(ks), (((2,), (2,)), ((0,), (0,))),
                        preferred_element_type=F32)
    s = s + jnp.stack(bs)
    m = jnp.max(s, axis=-1, keepdims=True)
    p = jnp.exp(s - m)
    den = jnp.sum(p, axis=-1, keepdims=True)
    o = lax.dot_general(p.astype(BF16), jnp.stack(vs), (((2,), (1,)), ((0,), (0,))),
                        preferred_element_type=F32) / den
    for d in range(rows_per_blk):
        o_ref[0, d * GRID_W:(d + 1) * GRID_W, :] = jnp.where(lo, o[2 * d], o[2 * d + 1]).astype(BF16)


def _na_bias_table(rpb):
    qc = np.arange(GRID_W)
    cstart = np.clip(qc - NA_COLS // 2, 0, GRID_W - NA_COLS)
    kc = np.arange(GRID_W)
    col_valid = (kc[None, :] >= cstart[:, None]) & (kc[None, :] < cstart[:, None] + NA_COLS)
    col_off = np.clip(kc[None, :] - qc[:, None] + NA_COLS - 1, 0, 2 * NA_COLS - 2)
    delta = np.arange(NA_ROWS)
    row_off = np.arange(NA_ROWS)[None, :] - delta[:, None] + NA_ROWS - 1
    t = rpb[:, row_off]
    t = t[:, :, :, col_off]
    t = jnp.where(jnp.asarray(col_valid)[None, None, None], t, NEG_INF)
    t = jnp.transpose(t, (0, 1, 3, 2, 4))
    h = rpb.shape[0]
    return t.reshape(h // 2, 2, NA_ROWS, GRID_W, NA_ROWS * GRID_W).astype(F32)


def _na_call(tk, qkv, bias):
    n = tk.n
    tq = NA_TQ
    nhp = N_HEADS_B // 2
    nblk = n // tq
    arr = qkv.reshape(3 * nhp, nblk, tq, LANES)
    nkeys = NA_ROWS * GRID_W

    arr3 = arr
    kern = functools.partial(_na_kernel, rows_p=tk.s_p // GRID_W, rows_s=tk.s_s // GRID_W,
                             blocks_p=tk.n_p // tq)

    def bspec(base, shift):
        return pl.BlockSpec((None, 1, tq, LANES),
                            lambda hp, b: (base + hp, jnp.clip(b + shift, 0, nblk - 1), 0, 0))

    return pl.pallas_call(
        kern,
        grid=(nhp, nblk),
        in_specs=[bspec(0, 0), bspec(nhp, -1), bspec(nhp, 0), bspec(nhp, 1),
                  bspec(2 * nhp, -1), bspec(2 * nhp, 0), bspec(2 * nhp, 1),
                  pl.BlockSpec((1, 2, NA_ROWS, GRID_W, nkeys), lambda hp, b: (hp, 0, 0, 0, 0))],
        out_specs=pl.BlockSpec((None, 1, tq, LANES), lambda hp, b: (hp, b, 0, 0)),
        out_shape=jax.ShapeDtypeStruct((nhp, nblk, tq, LANES), BF16),
        scratch_shapes=[pltpu.VMEM((3 * tq, LANES), BF16), pltpu.VMEM((3 * tq, LANES), BF16)],
        compiler_params=_cparams(("arbitrary", "arbitrary")),
        name="neighbourhood_attention",
    )(arr3, arr3, arr3, arr3, arr3, arr3, arr3, bias).reshape(nhp, n, LANES)


def _oproj_kernel(*refs, dils, n_tiles):
    n_groups = 1 if dils is None else len(dils)
    o_refs = refs[:n_groups]
    l_refs = refs[n_groups:2 * n_groups] if dils is not None else ()
    k = len(o_refs) + len(l_refs)
    (x_ref, wo_ref, g1_ref, ng_ref, sh_ref, sc_ref, wrh_ref, wrl_ref, br_ref,
     xo_ref, h_ref, lg_ref, m_ref) = refs[k:k + 13]
    slab = refs[k + 13] if dils is not None else None
    tm = x_ref.shape[0]
    n_strided = [0]

    def natural(ref, t, dil):
        if dil == 1:
            return ref[t, 0, 0].astype(F32)
        s = n_strided[0] % N_SLABS
        n_strided[0] += 1
        for p in range(dil):
            slab[s, pl.ds(p, tm // dil, stride=dil), :] = ref[t, 0, p].astype(F32)
        return slab[s]

    for t in range(n_tiles):
        if dils is None:
            m_ref[:, t * LANES:(t + 1) * LANES] = o_refs[0][t]
        else:
            ls = [natural(l, t, dil) for l, dil in zip(l_refs, dils)]
            mx = functools.reduce(jnp.maximum, ls)
            es = [jnp.exp(l - mx) for l in ls]
            den = functools.reduce(lambda a, b: a + b, es)
            num = functools.reduce(lambda a, b: a + b,
                                   [e * natural(o, t, dil) for e, o, dil in zip(es, o_refs, dils)])
            m_ref[:, t * LANES:(t + 1) * LANES] = (num / den).astype(BF16)
    mix = jnp.dot(m_ref[...], wo_ref[...], preferred_element_type=F32)
    x = x_ref[...] + g1_ref[0] * mix
    xo_ref[...] = x
    h = _norm_modulate(x, ng_ref[...], sh_ref[0], sc_ref[0])
    hi = h.astype(BF16)
    lo = (h - hi.astype(F32)).astype(BF16)
    tm, d = h.shape
    nsub = d // LANES
    for j in range(nsub):
        h_ref[pl.ds(j, tm, stride=nsub), :] = h[:, j * LANES:(j + 1) * LANES]
    lg = jnp.dot(hi, wrh_ref[...], preferred_element_type=F32)
    lg = lg + jnp.dot(lo, wrh_ref[...], preferred_element_type=F32)
    lg = lg + jnp.dot(hi, wrl_ref[...], preferred_element_type=F32)
    lg_ref[...] = lg + br_ref[...]


def _oproj_call(tk, os, ls, x, w_o, mod3, g1_j, ng, sh_j, sc_j, w_router, b_router, dils=None):
    n, d = x.shape
    tm = ROW_TILE
    n_tiles = os[0].shape[0]
    k_in = n_tiles * LANES
    if dils is None:
        tile_specs = [pl.BlockSpec((n_tiles, tm, LANES), lambda i: (0, i, 0))]
        scratch = [pltpu.VMEM((tm, k_in), BF16)]
    else:
        tpc = CHUNK // tm
        tile_specs = [pl.BlockSpec((n_tiles, 1, dil, tm // dil, LANES),
                                   lambda i: (0, i // tpc, 0, i % tpc, 0)) for dil in dils] * 2
        scratch = [pltpu.VMEM((tm, k_in), BF16), pltpu.VMEM((N_SLABS, tm, LANES), F32)]
    ne = w_router.shape[1]
    wr = jnp.zeros((d, ROUTER_PAD), F32).at[:, :ne].set(w_router)
    wr_hi = wr.astype(BF16)
    wr_lo = (wr - wr_hi.astype(F32)).astype(BF16)
    br = jnp.zeros((1, ROUTER_PAD), F32).at[0, :ne].set(b_router)
    brow = lambda i: tk.batch_row(i * tm)
    row_spec = pl.BlockSpec((tm, d), lambda i: (i, 0))
    mod_spec = lambda j: pl.BlockSpec((1, 1, d), lambda i: (brow(i), 0, j))
    kern = functools.partial(_oproj_kernel, dils=dils, n_tiles=n_tiles)
    return pl.pallas_call(
        kern,
        grid=(n // tm,),
        in_specs=tile_specs + [
            row_spec,
            pl.BlockSpec((k_in, d), lambda i: (0, 0)),
            mod_spec(g1_j),
            pl.BlockSpec((1, d), lambda i: (0, 0)),
            mod_spec(sh_j),
            mod_spec(sc_j),
            pl.BlockSpec((d, ROUTER_PAD), lambda i: (0, 0)),
            pl.BlockSpec((d, ROUTER_PAD), lambda i: (0, 0)),
            pl.BlockSpec((1, ROUTER_PAD), lambda i: (0, 0)),
        ],
        out_specs=[row_spec, pl.BlockSpec((tm * (d // LANES), LANES), lambda i: (i, 0)),
                   pl.BlockSpec((tm, ROUTER_PAD), lambda i: (i, 0))],
        out_shape=[jax.ShapeDtypeStruct((n, d), F32), jax.ShapeDtypeStruct((n * (d // LANES), LANES), F32),
                   jax.ShapeDtypeStruct((n, ROUTER_PAD), F32)],
        scratch_shapes=scratch,
        compiler_params=_cparams(("arbitrary",)),
        name="oproj_norm_router",
    )(*os, *ls, x, w_o, mod3, ng, mod3, mod3, wr_hi, wr_lo, br)


DMA_UNROLL = 16
ISSUE_SPLIT = (7, 12, 13)


def _moe_kernel(te_ref, nu_ref, gcur_ref, gnxt_ref, sprev_ref, h_hbm, wgu_ref, bgu_ref, wdn_ref,
                bdn_ref, y_hbm, xbuf, ybuf, acc_ref, wgu_b, wdn_b, sem_g, sem_s, *, d_ff, fchunk):
    t = pl.program_id(0)
    tm, d = acc_ref.shape
    nsub = d // LANES
    half = tm // 2
    n_used = nu_ref[0]
    n_chunks = d_ff // fchunk
    assert n_chunks == 4

    def row(ref, r):
        return ref.at[pl.ds(pl.multiple_of(r * nsub, nsub), nsub)]

    def gather_row(idx_ref, r, buf, prio=0):
        pltpu.make_async_copy(row(h_hbm, idx_ref[0, 0, r]), row(xbuf.at[buf], r),
                              sem_g.at[buf]).start(priority=prio)

    def scatter_row(idx_ref, r, buf, prio=1):
        pltpu.make_async_copy(row(ybuf.at[buf], r), row(y_hbm, idx_ref[0, 0, r]),
                              sem_s.at[buf]).start(priority=prio)

    def wait_gather_half(buf):
        pltpu.make_async_copy(h_hbm.at[pl.ds(0, half * nsub)], xbuf.at[buf, pl.ds(0, half * nsub)],
                              sem_g.at[buf]).wait()

    def wait_scatter_half(buf):
        pltpu.make_async_copy(ybuf.at[buf, pl.ds(0, half * nsub)], y_hbm.at[pl.ds(0, half * nsub)],
                              sem_s.at[buf]).wait()

    def rolled(issue):
        def body(i, carry):
            for j in range(DMA_UNROLL):
                issue(i * DMA_UNROLL + j)
            return carry
        lax.fori_loop(0, tm // DMA_UNROLL, body, 0)

    @pl.when(t == 0)
    def _():
        ybuf[...] = jnp.zeros_like(ybuf)
        spare = y_hbm.shape[0] - 2 * tm * nsub
        pltpu.make_async_copy(ybuf.at[0], y_hbm.at[pl.ds(spare, tm * nsub)], sem_s.at[0]).start()
        rolled(lambda r: gather_row(gcur_ref, r, 0))
        wait_gather_half(0)

    def step(s):
        o = 1 - s

        @pl.when((t == 0) | (te_ref[t] != te_ref[jnp.maximum(t - 1, 0)]))
        def _():
            rows = 128
            def cast(i, carry):
                r0 = pl.multiple_of(i * rows, rows)
                wgu_b[pl.ds(r0, rows), :] = wgu_ref[0, 0, pl.ds(r0, rows), :].astype(BF16)
                wdn_b[pl.ds(r0, rows), :] = wdn_ref[0, 0, pl.ds(r0, rows), :].astype(BF16)
                return carry
            lax.fori_loop(0, d // rows, cast, 0)

        wait_gather_half(s)
        x = jnp.concatenate([xbuf[s, pl.ds(j, tm, stride=nsub), :] for j in range(nsub)],
                            axis=1).astype(BF16)
        issues = []
        for r in range(tm):
            issues.append(("g", r))
            if r % 2 == 1:
                issues.append(("s", r // 2))
        issues += [("s", r) for r in range(half, tm)]
        n0 = ISSUE_SPLIT[0] * 2 * tm // sum(ISSUE_SPLIT)
        n1 = n0 + ISSUE_SPLIT[1] * 2 * tm // sum(ISSUE_SPLIT)
        groups = [issues[:n0], issues[n0:n1], issues[n1:], []]
        for c in range(n_chunks):
            lo, hi = c * fchunk, (c + 1) * fchunk
            if c == 1:
                wait_scatter_half(s)
            if c == 2:
                wait_scatter_half(s)
            if c == 3:
                wait_gather_half(o)
            gate = jnp.dot(x, wgu_b[:, lo:hi], preferred_element_type=F32) + bgu_ref[0, 0, :, lo:hi]
            up = jnp.dot(x, wgu_b[:, d_ff + lo:d_ff + hi],
                         preferred_element_type=F32) + bgu_ref[0, 0, :, d_ff + lo:d_ff + hi]
            gate = jnp.minimum(gate, SWIGLU_LIMIT)
            up = jnp.clip(up, -SWIGLU_LIMIT, SWIGLU_LIMIT)
            a = ((up + 1.0) * gate * jax.nn.sigmoid(SWIGLU_ALPHA * gate)).astype(BF16)
            part = jnp.dot(a, wdn_b[lo:hi, :], preferred_element_type=F32)
            if c == 0:
                acc_ref[...] = part + bdn_ref[0, 0]
            elif c < n_chunks - 1:
                acc_ref[...] += part
            else:
                for j in range(nsub):
                    cols = slice(j * LANES, (j + 1) * LANES)
                    ybuf[s, pl.ds(j, tm, stride=nsub), :] = acc_ref[:, cols] + part[:, cols]
            for k, (kind, r) in enumerate(groups[c]):
                if kind == "g":
                    gather_row(gnxt_ref, r, o, prio=k % 2)
                else:
                    scatter_row(sprev_ref, r, o, prio=k % 2)

    def drain(s):
        o = 1 - s
        wait_gather_half(s)
        rolled(lambda r: scatter_row(sprev_ref, r, o))
        for buf in (0, 1):
            wait_scatter_half(buf)
            wait_scatter_half(buf)

    pl.when(t < n_used)(functools.partial(step, t % 2))
    pl.when(t == n_used)(functools.partial(drain, t % 2))


def _moe_call(h, route, w_gu, b_gu, w_dn, b_dn, layer):
    depth, ne, d, d_ff2 = w_gu.shape
    nsub = d // LANES
    n = h.shape[0] // nsub
    d_ff = d_ff2 // 2
    tm = MOE_TILE
    n_tiles = route["tile_expert"].shape[0]
    kern = functools.partial(_moe_kernel, d_ff=d_ff, fchunk=256)
    idx_spec = lambda shift: pl.BlockSpec(
        (1, 1, tm), lambda t, te, nu: (jnp.minimum(t + shift, n_tiles - 1), 0, 0),
        memory_space=pltpu.SMEM)
    grid_spec = pltpu.PrefetchScalarGridSpec(
        num_scalar_prefetch=2,
        grid=(n_tiles,),
        in_specs=[
            idx_spec(0), idx_spec(1), idx_spec(0),
            pl.BlockSpec(memory_space=pl.ANY),
            pl.BlockSpec((1, 1, d, d_ff2), lambda t, te, nu: (layer, te[t], 0, 0)),
            pl.BlockSpec((1, 1, 1, d_ff2), lambda t, te, nu: (layer, te[t], 0, 0)),
            pl.BlockSpec((1, 1, d_ff, d), lambda t, te, nu: (layer, te[t], 0, 0)),
            pl.BlockSpec((1, 1, 1, d), lambda t, te, nu: (layer, te[t], 0, 0)),
        ],
        out_specs=pl.BlockSpec(memory_space=pl.ANY),
        scratch_shapes=[pltpu.VMEM((2, tm * nsub, LANES), F32), pltpu.VMEM((2, tm * nsub, LANES), F32),
                        pltpu.VMEM((tm, d), F32),
                        pltpu.VMEM((d, d_ff2), BF16), pltpu.VMEM((d_ff, d), BF16),
                        pltpu.SemaphoreType.DMA((2,)), pltpu.SemaphoreType.DMA((2,))],
    )
    return pl.pallas_call(
        kern,
        grid_spec=grid_spec,
        out_shape=jax.ShapeDtypeStruct(((TOP_K * n + 2 * tm) * nsub, LANES), F32),
        compiler_params=_cparams(("arbitrary",)),
        name="moe_experts",
    )(route["tile_expert"], route["n_used"], route["gsrc"], route["gsrc"], route["sdst_prev"], h,
      w_gu, b_gu.reshape(depth, ne, 1, d_ff2), w_dn, b_dn.reshape(depth, ne, 1, d))


def _route(logits, n):
    ne = N_EXPERTS
    tm = MOE_TILE
    top_val, top_idx = lax.top_k(logits[:, :ne], TOP_K)
    gates = jax.nn.softmax(top_val, axis=-1)
    e_flat = top_idx.reshape(-1).astype(jnp.int32)
    nk = n * TOP_K
    counts = jnp.sum((e_flat[:, None] == jnp.arange(ne, dtype=jnp.int32)[None, :]).astype(jnp.int32),
                     axis=0)
    padded = (counts + tm - 1) // tm * tm
    pad_ends = jnp.cumsum(padded)
    n_tiles = nk // tm + ne
    cap = n_tiles * tm
    tile_start = jnp.arange(n_tiles, dtype=jnp.int32) * tm
    tile_expert = jnp.minimum(jnp.sum((pad_ends[None, :] <= tile_start[:, None]).astype(jnp.int32), axis=1),
                              ne - 1).astype(jnp.int32)
    n_used = (pad_ends[-1] // tm).astype(jnp.int32).reshape(1)
    idx_bits = int(np.ceil(np.log2(nk + tm)))
    pad_base = (1 << idx_bits) - tm
    real_keys = (e_flat << idx_bits) + jnp.arange(nk, dtype=jnp.int32)
    lane = jnp.arange(tm, dtype=jnp.int32)[None, :]
    expert = jnp.arange(ne, dtype=jnp.int32)[:, None]
    pad_keys = jnp.where(lane < (padded - counts)[:, None], (expert << idx_bits) + pad_base + lane,
                         jnp.iinfo(jnp.int32).max)
    keys = jnp.concatenate([real_keys, pad_keys.reshape(-1)])
    low = lax.sort(keys, is_stable=False) & ((1 << idx_bits) - 1)
    valid = low < pad_base
    slot = jnp.arange(cap, dtype=jnp.int32)
    gsrc = jnp.where(valid, low // TOP_K, 0).astype(jnp.int32)
    spare = nk + ((slot // tm) % 2) * tm + slot % tm
    sdst = jnp.where(valid, (low % TOP_K) * n + low // TOP_K, spare).astype(jnp.int32)
    sdst = sdst.reshape(n_tiles, 1, tm)
    before_first = (nk + tm + jnp.arange(tm, dtype=jnp.int32)).reshape(1, 1, tm)
    sdst_prev = jnp.concatenate([before_first, sdst[:-1]], axis=0)
    return dict(gates=gates, tile_expert=tile_expert, n_used=n_used,
                gsrc=gsrc.reshape(n_tiles, 1, tm), sdst_prev=sdst_prev)


def _combine_kernel(x_ref, y0_ref, y1_ref, y2_ref, y3_ref, gt_ref, g2_ref, o_ref):
    tm, d = x_ref.shape
    nsub = d // LANES
    g = gt_ref[...]
    for j in range(nsub):
        y = None
        for k, y_ref in enumerate((y0_ref, y1_ref, y2_ref, y3_ref)):
            term = g[:, k:k + 1] * y_ref[pl.ds(j, tm, stride=nsub), :]
            y = term if y is None else y + term
        cols = slice(j * LANES, (j + 1) * LANES)
        o_ref[:, cols] = x_ref[:, cols] + g2_ref[0, :, cols] * y


def _combine_call(tk, x, ycomb, gates, mod3, g2_j):
    n, d = x.shape
    nsub = d // LANES
    tm = COMBINE_TILE
    nt = n // tm
    brow = lambda i: tk.batch_row(i * tm)
    row_spec = pl.BlockSpec((tm, d), lambda i: (i, 0))
    y_spec = lambda k: pl.BlockSpec((tm * nsub, LANES), lambda i: (k * nt + i, 0))
    return pl.pallas_call(
        _combine_kernel,
        grid=(nt,),
        in_specs=[row_spec, y_spec(0), y_spec(1), y_spec(2), y_spec(3),
                  pl.BlockSpec((tm, TOP_K), lambda i: (i, 0)),
                  pl.BlockSpec((1, 1, d), lambda i: (brow(i), 0, g2_j))],
        out_specs=row_spec,
        out_shape=jax.ShapeDtypeStruct((n, d), F32),
        compiler_params=_cparams(("arbitrary",)),
        name="moe_combine",
    )(x, ycomb, ycomb, ycomb, ycomb, gates, mod3)


def _moe_experts(tk, h, logits, w_gu, b_gu, w_dn, b_dn, layer):
    route = _route(logits, logits.shape[0])
    return _moe_call(h, route, w_gu, b_gu, w_dn, b_dn, layer), route["gates"]


@jax.jit
def _forward(x_prompt, x_sample, c_prompt, c_sample, norm_g, w_mod, b_mod, a_w_qkv, a_q_gain,
             a_k_gain, a_w_o, b_w_qkv, b_q_gain, b_k_gain, b_rpb, b_w_o, w_router, b_router,
             w_gu, b_gu, w_dn, b_dn):
    b_p, s_p, d = x_prompt.shape
    b_s, s_s, _ = x_sample.shape
    tk = _Tokens(b_p, s_p, b_s, s_s)
    depth = w_mod.shape[0]
    x = jnp.concatenate([x_prompt.reshape(-1, d), x_sample.reshape(-1, d)], axis=0)

    nb = b_p + b_s
    rows = -(-nb // 8) * 8
    c_all = jnp.zeros((rows, d), F32).at[:nb].set(jnp.concatenate([c_prompt, c_sample], axis=0))
    mods = _modulation(c_all, w_mod, b_mod)

    cos, sin = _rope_tables(tk.max_s)
    kinds_a = tuple("qkv"[cb // (len(DILATIONS) * HEADS_PER_GROUP_A // 2)]
                    for cb in range(3 * len(DILATIONS) * HEADS_PER_GROUP_A // 2))
    kinds_b = tuple("qkv"[cb // (N_HEADS_B // 2)] for cb in range(3 * N_HEADS_B // 2))

    for layer in range(depth):
        mod3 = mods[layer].reshape(rows, 1, 6 * d)
        j = layer // 2
        if layer % 2 == 0:
            qkv = _qkv_call(tk, x, norm_g[layer, 0].reshape(1, d), mod3, 0, 1,
                            a_w_qkv[j].astype(BF16), a_q_gain[j], a_k_gain[j], cos, sin, kinds_a, True,
                            dils=DILATIONS)
            groups = _dilated_attention(tk, qkv)
            os = [g[0] for g in groups]
            ls = [g[1] for g in groups]
            w_o = a_w_o[j]
            dils = DILATIONS
        else:
            dils = None
            qkv = _qkv_call(tk, x, norm_g[layer, 0].reshape(1, d), mod3, 0, 1,
                            b_w_qkv[j].astype(BF16), b_q_gain[j], b_k_gain[j], cos, sin, kinds_b, False)
            os = [_na_call(tk, qkv, _na_bias_table(b_rpb[j]))]
            ls = []
            w_o = b_w_o[j]
        x, hff, logits = _oproj_call(tk, os, ls, x, w_o.astype(BF16), mod3, 2,
                                     norm_g[layer, 1].reshape(1, d), 3, 4,
                                     w_router[layer], b_router[layer], dils=dils)
        ycomb, gates = _moe_experts(tk, hff, logits, w_gu, b_gu, w_dn, b_dn, layer)
        x = _combine_call(tk, x, ycomb, gates, mod3, 5)

    y_prompt = x[:tk.n_p].reshape(b_p, s_p, d)
    y_sample = x[tk.n_p:].reshape(b_s, s_s, d)
    return y_prompt, y_sample


def kernel(x_prompt, x_sample, c_prompt, c_sample, norm_g, w_mod, b_mod, a_w_qkv, a_q_gain, a_k_gain,
           a_w_o, b_w_qkv, b_q_gain, b_k_gain, b_rpb, b_w_o, w_router, b_router, w_gu, b_gu, w_dn, b_dn):
    return _forward(x_prompt, x_sample, c_prompt, c_sample, norm_g, w_mod, b_mod, a_w_qkv, a_q_gain,
                    a_k_gain, a_w_o, b_w_qkv, b_q_gain, b_k_gain, b_rpb, b_w_o, w_router, b_router,
                    w_gu, b_gu, w_dn, b_dn)
```
